```python
import math
import jax, jax.numpy as jnp
from jax import lax
import numpy as np

D_MODEL = 1024
BATCH = 8
SEQ = 4096
DEPTH = 1
DEC_BATCH = 32
DEC_SEQ = 4
PAST_LEN = 16384
PAGE_SIZE = 128

H_DIFF = 4
DH_DIFF = D_MODEL // (4 * H_DIFF)
DV_DIFF = 2 * DH_DIFF
W_DIFF = H_DIFF * DV_DIFF
H_RET = 4
DK_RET = D_MODEL // (2 * H_RET)
DV_RET = D_MODEL // (2 * H_RET)
W_RET = H_RET * DV_RET
IN_SPLITS = (2 * H_DIFF * DH_DIFF, 2 * H_DIFF * DH_DIFF, H_DIFF * DV_DIFF,
             H_RET * DK_RET, H_RET * DK_RET, H_RET * DV_RET, H_RET * DV_RET)
IN_COLS = 4 * H_DIFF * DH_DIFF + H_DIFF * DV_DIFF + 2 * H_RET * DK_RET + 2 * H_RET * DV_RET
Q_BLOCK = 128
RET_CHUNK = 128
ROPE_THETA = 10000.0
N_GROUPS = 4
EXPERTS_PER_GROUP = 4
N_EXPERTS = N_GROUPS * EXPERTS_PER_GROUP
D_EXPERT = D_MODEL // 4
MOE_TOPK = 2
NORM_EPS = 1e-6
GN_EPS = 1e-5

kernel_name = "hybrid_diffattn_retention_hmoe_decode_step"


def rmsnorm(x, g):
    xf = x.astype(jnp.float32)
    y = xf * lax.rsqrt(jnp.mean(xf * xf, axis=-1, keepdims=True) + NORM_EPS)
    return (y * g.astype(jnp.float32)).astype(x.dtype)


def rope(x, pos):
    d = x.shape[-1]
    half = d // 2
    inv = jnp.power(ROPE_THETA, -(2.0 / d) * jnp.arange(half, dtype=jnp.float32))
    ang = pos[:, None] * inv[None, :]
    cos = jnp.cos(ang)[None, :, None, :]
    sin = jnp.sin(ang)[None, :, None, :]
    xf = x.astype(jnp.float32)
    x1, x2 = xf[..., :half], xf[..., half:]
    return jnp.concatenate([x1 * cos - x2 * sin, x2 * cos + x1 * sin], axis=-1).astype(x.dtype)


def adaln(c, w_ada, b_ada):
    m = (jax.nn.silu(c) @ w_ada + b_ada)[:, None, :]
    return jnp.split(m, 6, axis=-1)


def mixer_inputs(h, pos, w_in):
    B, T, _ = h.shape
    z = h @ w_in
    idx = np.cumsum(IN_SPLITS)[:-1].tolist()
    qd, kd, vd, qr, kr, vr, gr = jnp.split(z, idx, axis=-1)
    qd = rope(qd.reshape(B, T, 2 * H_DIFF, DH_DIFF), pos) * (DH_DIFF ** -0.5)
    kd = rope(kd.reshape(B, T, 2 * H_DIFF, DH_DIFF), pos)
    vd = vd.reshape(B, T, H_DIFF, DV_DIFF)
    qr = rope(qr.reshape(B, T, H_RET, DK_RET), pos)
    kr = rope(kr.reshape(B, T, H_RET, DK_RET), pos) * (DK_RET ** -0.5)
    vr = vr.reshape(B, T, H_RET, DV_RET)
    return qd, kd, vd, qr, kr, vr, gr


def diff_lambda(lq1, lk1, lq2, lk2, lam_init):
    e1 = jnp.exp(jnp.sum(lq1.astype(jnp.float32) * lk1.astype(jnp.float32)))
    e2 = jnp.exp(jnp.sum(lq2.astype(jnp.float32) * lk2.astype(jnp.float32)))
    return e1 - e2 + lam_init


def diff_attn_blocked(q, k, v, lam):
    B, T = q.shape[:2]
    nb = T // Q_BLOCK
    qb = q.reshape(B, nb, Q_BLOCK, 2 * H_DIFF, DH_DIFF).swapaxes(0, 1)
    kpos = jnp.arange(T)

    def one_block(args):
        qblk, start = args
        s = jnp.einsum('bqhd,bkhd->bhqk', qblk, k).astype(jnp.float32)
        qpos = start + jnp.arange(Q_BLOCK)
        s = jnp.where(kpos[None, :] <= qpos[:, None], s, -jnp.inf)
        p = jax.nn.softmax(s, axis=-1).reshape(B, H_DIFF, 2, Q_BLOCK, T)
        a = p[:, :, 0] - lam * p[:, :, 1]
        return jnp.einsum('bhqk,bkhe->bqhe', a.astype(v.dtype), v)

    o = lax.map(one_block, (qb, jnp.arange(nb, dtype=jnp.int32) * Q_BLOCK))
    return o.swapaxes(0, 1).reshape(B, T, H_DIFF, DV_DIFF)


def diff_attn_paged(q, k_new, v_new, k_past, v_past, lam):
    B, T = q.shape[:2]
    P = k_past.shape[1]
    s_past = jnp.einsum('bqhd,bkhd->bhqk', q, k_past).astype(jnp.float32)
    s_new = jnp.einsum('bqhd,bkhd->bhqk', q, k_new).astype(jnp.float32)
    s_new = jnp.where(jnp.tril(jnp.ones((T, T), dtype=bool)), s_new, -jnp.inf)
    p = jax.nn.softmax(jnp.concatenate([s_past, s_new], axis=-1), axis=-1)
    p = p.reshape(B, H_DIFF, 2, T, P + T)
    a = (p[:, :, 0] - lam * p[:, :, 1]).astype(v_new.dtype)
    return (jnp.einsum('bhqk,bkhe->bqhe', a[..., :P], v_past)
            + jnp.einsum('bhqk,bkhe->bqhe', a[..., P:], v_new))


def retention(q, k, v, r0, chunk):
    B, T, H, DK = q.shape
    DV = v.shape[-1]
    n = T // chunk
    log_g = jnp.log1p(-jnp.exp2(-5.0 - jnp.arange(H, dtype=jnp.float32)))
    idx = jnp.arange(chunk, dtype=jnp.float32)
    dif = idx[:, None] - idx[None, :]
    decay_intra = jnp.where(dif >= 0, jnp.exp(jnp.maximum(dif, 0.0)[None] * log_g[:, None, None]), 0.0)
    q_dec = jnp.exp((idx[None, :] + 1.0) * log_g[:, None])
    k_dec = jnp.exp((chunk - 1.0 - idx[None, :]) * log_g[:, None])
    chunk_dec = jnp.exp(chunk * log_g)
    qc = q.astype(jnp.float32).reshape(B, n, chunk, H, DK)
    kc = k.astype(jnp.float32).reshape(B, n, chunk, H, DK)
    vc = v.astype(jnp.float32).reshape(B, n, chunk, H, DV)
    scores = jnp.einsum('bnihd,bnjhd->bnhij', qc, kc) * decay_intra
    o_intra = jnp.einsum('bnhij,bnjhe->bnihe', scores, vc)
    kv = jnp.einsum('bnjhd,bnjhe,hj->nbhde', kc, vc, k_dec)

    def step(r, kv_n):
        return r * chunk_dec[None, :, None, None] + kv_n, r

    r_final, r_prev = lax.scan(step, r0.astype(jnp.float32), kv)
    o_cross = jnp.einsum('bnihd,nbhde,hi->bnihe', qc, r_prev, q_dec)
    return (o_intra + o_cross).reshape(B, T, H, DV), r_final


def mixer_output(od, orr, gr, beta, lam_init, w_out):
    B, T = od.shape[:2]
    of = od.astype(jnp.float32)
    of = of * lax.rsqrt(jnp.mean(of * of, axis=-1, keepdims=True) + NORM_EPS) * (1.0 - lam_init)
    mu = jnp.mean(orr, axis=-1, keepdims=True)
    var = jnp.mean(jnp.square(orr - mu), axis=-1, keepdims=True)
    rf = ((orr - mu) * lax.rsqrt(var + GN_EPS)).reshape(B, T, W_RET) * jax.nn.silu(gr.astype(jnp.float32))
    y = jnp.concatenate([of.reshape(B, T, W_DIFF), rf], axis=-1) * beta.astype(jnp.float32)
    return y.astype(w_out.dtype) @ w_out


def hier_moe(h, w_group, b_group, w_er, b_er, w_g, w_u, w_d):
    B, T, D = h.shape
    x = h.reshape(B * T, D)
    lg = (x @ w_group + b_group).astype(jnp.float32)
    pg = jax.nn.softmax(lg, axis=-1)
    gsel = jnp.argmax(lg, axis=-1)
    pg_sel = jnp.take_along_axis(pg, gsel[:, None], axis=1)
    le_all = (jnp.einsum('nd,gde->nge', x, w_er) + b_er).astype(jnp.float32)
    le = jnp.take_along_axis(le_all, gsel[:, None, None], axis=1)[:, 0]
    top_v, top_i = lax.top_k(le, MOE_TOPK)
    w_top = jax.nn.softmax(top_v, axis=-1) * pg_sel
    e_idx = gsel[:, None] * EXPERTS_PER_GROUP + top_i
    gates = jnp.einsum('nk,nke->ne', w_top, jax.nn.one_hot(e_idx, N_EXPERTS, dtype=jnp.float32))
    a = jnp.einsum('nd,edf->nef', x, w_g)
    u = jnp.einsum('nd,edf->nef', x, w_u)
    hh = jax.nn.silu(a) * u * gates[..., None].astype(x.dtype)
    return jnp.einsum('nef,efd->nd', hh, w_d).reshape(B, T, D)


def run_layer(x, c, pos, attend, r0, ret_chunk, w_ada, b_ada, g_mix, g_ffn, w_in, lam, lam_init,
              beta, w_out, w_group, b_group, w_er, b_er, w_g, w_u, w_d):
    sh1, sc1, gt1, sh2, sc2, gt2 = adaln(c, w_ada, b_ada)
    h = rmsnorm(x, g_mix) * (1.0 + sc1) + sh1
    qd, kd, vd, qr, kr, vr, gr = mixer_inputs(h, pos, w_in)
    od = attend(qd, kd, vd, lam)
    orr, r_new = retention(qr, kr, vr, r0, ret_chunk)
    x = x + gt1 * mixer_output(od, orr, gr, beta, lam_init, w_out)
    h = rmsnorm(x, g_ffn) * (1.0 + sc2) + sh2
    x = x + gt2 * hier_moe(h, w_group, b_group, w_er, b_er, w_g, w_u, w_d)
    return x, kd, vd, r_new


def setup_inputs(seed: int = 0) -> dict:
    key = jax.random.key(seed)
    ks = jax.random.split(key, 27)

    def nrm(k, shape, scale):
        return jax.random.normal(k, shape, jnp.float32) * scale

    n_pages = PAST_LEN // PAGE_SIZE
    n_used = DEC_BATCH * n_pages
    n_phys = n_used + max(1, n_used // 4)
    page_table = jax.random.permutation(ks[5], n_phys)[:n_used].reshape(DEC_BATCH, n_pages).astype(jnp.int32)
    D = D_MODEL
    return {
        "x_prompt": nrm(ks[0], (BATCH, SEQ, D), 1.0),
        "x_sample": nrm(ks[1], (DEC_BATCH, DEC_SEQ, D), 1.0),
        "cache_k": nrm(ks[2], (DEPTH, n_phys, PAGE_SIZE, 2 * H_DIFF, DH_DIFF), 1.0),
        "cache_v": nrm(ks[3], (DEPTH, n_phys, PAGE_SIZE, H_DIFF, DV_DIFF), 1.0),
        "state_ret": nrm(ks[4], (DEPTH, DEC_BATCH, H_RET, DK_RET, DV_RET), 0.5),
        "page_table": page_table,
        "c_prompt": nrm(ks[6], (BATCH, D), 1.0),
        "c_sample": nrm(ks[7], (DEC_BATCH, D), 1.0),
        "w_ada": nrm(ks[8], (DEPTH, D, 6 * D), 0.5 * D ** -0.5),
        "b_ada": nrm(ks[9], (DEPTH, 6 * D), 0.01),
        "norm_mix_g": 1.0 + nrm(ks[10], (DEPTH, D), 0.02),
        "norm_ffn_g": 1.0 + nrm(ks[11], (DEPTH, D), 0.02),
        "w_in": nrm(ks[12], (DEPTH, D, IN_COLS), D ** -0.5),
        "lambda_q1": nrm(ks[13], (DEPTH, DH_DIFF), 0.1),
        "lambda_k1": nrm(ks[14], (DEPTH, DH_DIFF), 0.1),
        "lambda_q2": nrm(ks[15], (DEPTH, DH_DIFF), 0.1),
        "lambda_k2": nrm(ks[16], (DEPTH, DH_DIFF), 0.1),
        "beta_mix": 1.0 + nrm(ks[17], (DEPTH, D), 0.02),
        "w_out": nrm(ks[18], (DEPTH, D, D), D ** -0.5),
        "w_group": nrm(ks[19], (DEPTH, D, N_GROUPS), D ** -0.5),
        "b_group": nrm(ks[20], (DEPTH, N_GROUPS), 0.01),
        "w_expert_router": nrm(ks[21], (DEPTH, N_GROUPS, D, EXPERTS_PER_GROUP), D ** -0.5),
        "b_expert_router": nrm(ks[22], (DEPTH, N_GROUPS, EXPERTS_PER_GROUP), 0.01),
        "w_gate_e": nrm(ks[23], (DEPTH, N_EXPERTS, D, D_EXPERT), D ** -0.5),
        "w_up_e": nrm(ks[24], (DEPTH, N_EXPERTS, D, D_EXPERT), D ** -0.5),
        "w_down_e": nrm(ks[25], (DEPTH, N_EXPERTS, D_EXPERT, D), D_EXPERT ** -0.5),
        "final_g": 1.0 + nrm(ks[26], (D,), 0.02),
    }


def reference(x_prompt, x_sample, cache_k, cache_v, state_ret, page_table, c_prompt, c_sample,
              w_ada, b_ada, norm_mix_g, norm_ffn_g, w_in, lambda_q1, lambda_k1, lambda_q2, lambda_k2,
              beta_mix, w_out, w_group, b_group, w_expert_router, b_expert_router,
              w_gate_e, w_up_e, w_down_e, final_g):
    Bp, Tp = x_prompt.shape[:2]
    Bs, Ts = x_sample.shape[:2]
    n_pages = page_table.shape[1]
    past_len = n_pages * PAGE_SIZE
    pos_p = jnp.arange(Tp, dtype=jnp.float32)
    pos_s = past_len + jnp.arange(Ts, dtype=jnp.float32)
    xp, xs = x_prompt, x_sample
    kps, vps, rps, kss, vss, rss = [], [], [], [], [], []
    for l in range(DEPTH):
        lam_init = 0.8 - 0.6 * math.exp(-0.3 * l)
        lam = diff_lambda(lambda_q1[l], lambda_k1[l], lambda_q2[l], lambda_k2[l], lam_init)
        shared = (w_ada[l], b_ada[l], norm_mix_g[l], norm_ffn_g[l], w_in[l], lam, lam_init,
                  beta_mix[l], w_out[l], w_group[l], b_group[l], w_expert_router[l], b_expert_router[l],
                  w_gate_e[l], w_up_e[l], w_down_e[l])
        r0p = jnp.zeros((Bp, H_RET, DK_RET, DV_RET), jnp.float32)
        xp, kp, vp, rp = run_layer(xp, c_prompt, pos_p, diff_attn_blocked, r0p, RET_CHUNK, *shared)
        k_past = cache_k[l, page_table].reshape(Bs, past_len, 2 * H_DIFF, DH_DIFF)
        v_past = cache_v[l, page_table].reshape(Bs, past_len, H_DIFF, DV_DIFF)
        attend_s = lambda q, k, v, lm: diff_attn_paged(q, k, v, k_past, v_past, lm)
        xs, ks_, vs_, rs = run_layer(xs, c_sample, pos_s, attend_s, state_ret[l], Ts, *shared)
        kps.append(kp); vps.append(vp); rps.append(rp.astype(state_ret.dtype))
        kss.append(ks_); vss.append(vs_); rss.append(rs.astype(state_ret.dtype))
    y_prompt = rmsnorm(xp, final_g)
    y_sample = rmsnorm(xs, final_g)
    return (y_prompt, y_sample, jnp.stack(kps), jnp.stack(vps), jnp.stack(rps),
            jnp.stack(kss), jnp.stack(vss), jnp.stack(rss))
```

```python
import functools
import math

import jax
import jax.numpy as jnp
import numpy as np
from jax import lax
from jax.experimental import pallas as pl
from jax.experimental.pallas import tpu as pltpu

F32 = jnp.float32
BF16 = jnp.bfloat16

V7X_LANES = 128
V7X_VMEM_BYTES = 64 * 1024 * 1024
VMEM_LIMIT = 56 * 1024 * 1024

D_MODEL = 1024
PAGE_SIZE = 128
H_DIFF = 4
DH_DIFF = 64
DV_DIFF = 128
W_DIFF = H_DIFF * DV_DIFF
H_RET = 4
DK_RET = 128
DV_RET = 128
W_RET = H_RET * DV_RET
IN_COLS = 7 * 512
RET_CHUNK = 128
ROPE_THETA = 10000.0
N_GROUPS = 4
EXPERTS_PER_GROUP = 4
N_EXPERTS = 16
D_EXPERT = 256
NORM_EPS = 1e-6
GN_EPS = 1e-5
LAM_INIT = 0.8 - 0.6 * math.exp(-0.3 * 0)
LOG2E = 1.4426950408889634
NEG_BIG = -1e30
LOG_G = tuple(math.log1p(-2.0 ** (-5.0 - h)) for h in range(H_RET))
ROUTER_E0 = N_GROUPS


def _silu(x):
    return x * jax.nn.sigmoid(x)


def _adaln_kernel(c_ref, w_ref, b_ref, o_ref):
    s = _silu(c_ref[...])
    o_ref[...] = jnp.dot(s, w_ref[...], preferred_element_type=F32,
                         precision=lax.Precision.HIGHEST) + b_ref[...]


def _adaln(c, w_ada, b_ada):
    n, d = c.shape
    cols = w_ada.shape[1]
    bn = 1024
    return pl.pallas_call(
        _adaln_kernel,
        grid=(cols // bn,),
        in_specs=[pl.BlockSpec((n, d), lambda j: (0, 0)),
                  pl.BlockSpec((d, bn), lambda j: (0, j)),
                  pl.BlockSpec((1, bn), lambda j: (0, j))],
        out_specs=pl.BlockSpec((n, bn), lambda j: (0, j)),
        out_shape=jax.ShapeDtypeStruct((n, cols), F32),
        name="adaln",
    )(c, w_ada, b_ada.reshape(1, cols))


def _rope64(z, cos, sin_signed, first_half):
    rot = jnp.where(first_half, pltpu.roll(z, 96, 1), pltpu.roll(z, 32, 1))
    return z * cos + rot * sin_signed


def _rope128(z, cos, sin_signed):
    return z * cos + pltpu.roll(z, 64, 1) * sin_signed


def _inproj_kernel(x_ref, g_ref, sc_ref, sh_ref, w_ref, cd_ref, sd_ref, cr_ref, sr_ref,
                   kout_ref, vout_ref, q_ref, kb_ref, v_ref, qr_ref, kr_ref, vr_ref, gr_ref,
                   *, transposed):
    x = x_ref[0]
    y = x * lax.rsqrt(jnp.mean(x * x, axis=-1, keepdims=True) + NORM_EPS)
    h = (y * g_ref[...]) * (1.0 + sc_ref[0]) + sh_ref[0]
    hb = h.astype(BF16)
    tm = x.shape[0]

    def proj(section):
        return jnp.dot(hb, w_ref[:, section * 512:(section + 1) * 512], preferred_element_type=F32)

    zq, zk, zv, zqr, zkr, zvr, zgr = [proj(i) for i in range(7)]

    cd, sd, cr, sr = cd_ref[...], sd_ref[...], cr_ref[...], sr_ref[...]
    lane = lax.broadcasted_iota(jnp.int32, (tm, V7X_LANES), 1)
    first_half = (lane % DH_DIFF) < (DH_DIFF // 2)
    q_scale = (DH_DIFF ** -0.5) * LOG2E
    for j in range(4):
        cols = slice(j * V7X_LANES, (j + 1) * V7X_LANES)
        q = _rope64(zq[:, cols], cd, sd, first_half) * q_scale
        k = _rope64(zk[:, cols], cd, sd, first_half)
        v = zv[:, cols]
        kout_ref[0, :, cols] = k
        vout_ref[0, :, cols] = v
        kb_ref[0, :, cols] = k.astype(BF16)
        if transposed:
            q_ref[0, cols, :] = q.T.astype(BF16)
            v_ref[0, cols, :] = v.T.astype(BF16)
        else:
            q_ref[0, :, cols] = q.astype(BF16)
            v_ref[0, :, cols] = v.astype(BF16)
        qr_ref[0, :, cols] = _rope128(zqr[:, cols], cr, sr).astype(BF16)
        kr_ref[0, :, cols] = (_rope128(zkr[:, cols], cr, sr) * (DK_RET ** -0.5)).astype(BF16)
        vr_ref[0, :, cols] = zvr[:, cols].astype(BF16)
        gr_ref[0, :, cols] = zgr[:, cols].astype(BF16)


def _inproj(x, g, sc, sh, w_in_b, tables, *, tm, transposed):
    bx, tx, d = x.shape
    r = sc.shape[1]
    nt = tx // tm
    mod_spec = pl.BlockSpec((1, r, d), (lambda b, t: (b, 0, 0)) if r == 1 else (lambda b, t: (b, t, 0)))
    tab_spec = pl.BlockSpec((tm, V7X_LANES), lambda b, t: (t, 0))
    nat = lambda dt: jax.ShapeDtypeStruct((bx, tx, 512), dt)
    nat_spec = pl.BlockSpec((1, tm, 512), lambda b, t: (b, t, 0))
    if transposed:
        tr = jax.ShapeDtypeStruct((bx, 512, tx), BF16)
        tr_spec = pl.BlockSpec((1, 512, tm), lambda b, t: (b, 0, t))
    else:
        tr, tr_spec = nat(BF16), nat_spec
    out_shape = [nat(F32), nat(F32), tr, nat(BF16), tr, nat(BF16), nat(BF16), nat(BF16), nat(BF16)]
    out_specs = [nat_spec, nat_spec, tr_spec, nat_spec, tr_spec, nat_spec, nat_spec, nat_spec, nat_spec]
    return pl.pallas_call(
        functools.partial(_inproj_kernel, transposed=transposed),
        grid=(bx, nt),
        in_specs=[pl.BlockSpec((1, tm, d), lambda b, t: (b, t, 0)),
                  pl.BlockSpec((1, d), lambda b, t: (0, 0)),
                  mod_spec, mod_spec,
                  pl.BlockSpec((d, IN_COLS), lambda b, t: (0, 0)),
                  tab_spec, tab_spec, tab_spec, tab_spec],
        out_specs=out_specs,
        out_shape=out_shape,
        compiler_params=pltpu.CompilerParams(
            dimension_semantics=("arbitrary", "arbitrary"), vmem_limit_bytes=VMEM_LIMIT),
        name="inproj",
    )(x, g, sc, sh, w_in_b, *tables)


def _rope_tables(pos):
    lane = np.arange(V7X_LANES)

    def one(d):
        half = d // 2
        inv = jnp.power(ROPE_THETA, -(2.0 / d) * jnp.arange(half, dtype=F32))
        ang = pos[:, None] * inv[None, :]
        idx = lane % half
        sign = jnp.asarray(np.where((lane % d) < half, -1.0, 1.0), F32)
        return jnp.cos(ang)[:, idx], jnp.sin(ang)[:, idx] * sign[None, :]

    cd, sd = one(DH_DIFF)
    cr, sr = one(DK_RET)
    return cd, sd, cr, sr


def _diff_lambda(lq1_ref, lk1_ref, lq2_ref, lk2_ref):
    e1 = jnp.exp(jnp.sum(lq1_ref[...] * lk1_ref[...], axis=-1, keepdims=True))
    e2 = jnp.exp(jnp.sum(lq2_ref[...] * lk2_ref[...], axis=-1, keepdims=True))
    return e1 - e2 + LAM_INIT


def _attn_kernel(qT_ref, k_ref, vT_ref, lq1_ref, lk1_ref, lq2_ref, lk2_ref, o_ref,
                 m_sc, l_sc, acc_sc, *, tq, tk):
    qi = pl.program_id(2)
    qT = qT_ref[0]
    row = lax.broadcasted_iota(jnp.int32, qT.shape, 0)
    zero = jnp.zeros_like(qT)
    q_sub = (jnp.where(row < DH_DIFF, qT, zero), jnp.where(row >= DH_DIFF, qT, zero))
    m_sc[...] = jnp.full(m_sc.shape, NEG_BIG, F32)
    l_sc[...] = jnp.zeros(l_sc.shape, F32)
    acc_sc[...] = jnp.zeros(acc_sc.shape, F32)

    def step(j, masked):
        k0 = pl.multiple_of(j * tk, tk)
        kblk = k_ref[0, pl.ds(k0, tk), :]
        vblk = vT_ref[0, :, pl.ds(k0, tk)]
        if masked:
            kpos = k0 + lax.broadcasted_iota(jnp.int32, (tk, tq), 0)
            qpos = qi * tq + lax.broadcasted_iota(jnp.int32, (tk, tq), 1)
            keep = kpos <= qpos
        for a in range(2):
            s = jnp.dot(kblk, q_sub[a], preferred_element_type=F32)
            if masked:
                s = jnp.where(keep, s, NEG_BIG)
            m_old = m_sc[a]
            m_new = jnp.maximum(m_old, jnp.max(s, axis=0, keepdims=True))
            alpha = jnp.exp2(m_old - m_new)
            p = jnp.exp2(s - m_new)
            l_sc[a] = alpha * l_sc[a] + jnp.sum(p, axis=0, keepdims=True)
            acc_sc[a] = acc_sc[a] * alpha + jnp.dot(vblk, p.astype(BF16), preferred_element_type=F32)
            m_sc[a] = m_new

    n_full = qi * (tq // tk)
    lax.fori_loop(0, n_full, lambda j, c: (step(j, False), c)[1], 0)
    for d in range(tq // tk):
        step(n_full + d, True)

    lam = _diff_lambda(lq1_ref, lk1_ref, lq2_ref, lk2_ref)
    o = acc_sc[0] / l_sc[0] - lam * (acc_sc[1] / l_sc[1])
    of = o * lax.rsqrt(jnp.mean(o * o, axis=0, keepdims=True) + NORM_EPS) * (1.0 - LAM_INIT)
    o_ref[0] = of.T.astype(o_ref.dtype)


def _attn_prompt(qT, kb, vT, lams, *, tq, tk):
    b, _, t = qT.shape
    lam_spec = pl.BlockSpec((1, DH_DIFF), lambda b_, h, i: (0, 0))
    return pl.pallas_call(
        functools.partial(_attn_kernel, tq=tq, tk=tk),
        grid=(b, H_DIFF, t // tq),
        in_specs=[pl.BlockSpec((1, V7X_LANES, tq), lambda b_, h, i: (b_, h, i)),
                  pl.BlockSpec((1, t, V7X_LANES), lambda b_, h, i: (b_, 0, h)),
                  pl.BlockSpec((1, V7X_LANES, t), lambda b_, h, i: (b_, h, 0)),
                  lam_spec, lam_spec, lam_spec, lam_spec],
        out_specs=pl.BlockSpec((1, tq, V7X_LANES), lambda b_, h, i: (b_, i, h)),
        out_shape=jax.ShapeDtypeStruct((b, t, W_DIFF), BF16),
        scratch_shapes=[pltpu.VMEM((2, 1, tq), F32), pltpu.VMEM((2, 1, tq), F32),
                        pltpu.VMEM((2, DV_DIFF, tq), F32)],
        compiler_params=pltpu.CompilerParams(
            dimension_semantics=("arbitrary", "arbitrary", "arbitrary"), vmem_limit_bytes=VMEM_LIMIT),
        name="attn_prompt",
    )(qT, kb, vT, *lams)


def _decode_kernel(pt_ref, q_ref, kn_ref, vn_ref, lq1_ref, lk1_ref, lq2_ref, lk2_ref, *rest, pages, ts):
    k_refs, v_refs = rest[:pages], rest[pages:2 * pages]
    o_ref, m_sc, l_sc, acc_sc = rest[2 * pages:]
    s_id = pl.program_id(1)
    q = q_ref[0]
    nt = (((1,), (1,)), ((), ()))

    @pl.when(s_id == 0)
    def _():
        m_sc[...] = jnp.full(m_sc.shape, NEG_BIG, F32)
        l_sc[...] = jnp.zeros(l_sc.shape, F32)
        acc_sc[...] = jnp.zeros(acc_sc.shape, F32)

    def update(s, v_list):
        m_old = m_sc[:, 0:1]
        m_new = jnp.maximum(m_old, jnp.max(s, axis=1, keepdims=True))
        alpha = jnp.exp2(m_old - m_new)
        p = jnp.exp2(s - m_new)
        l_new = alpha * l_sc[:, 0:1] + jnp.sum(p, axis=1, keepdims=True)
        pv = None
        for i, v in enumerate(v_list):
            part = jnp.dot(p[:, i * PAGE_SIZE:(i + 1) * PAGE_SIZE].astype(BF16), v,
                           preferred_element_type=F32)
            pv = part if pv is None else pv + part
        acc_sc[...] = acc_sc[...] * alpha + pv
        m_sc[...] = jnp.broadcast_to(m_new, m_sc.shape)
        l_sc[...] = jnp.broadcast_to(l_new, l_sc.shape)

    s_past = jnp.concatenate(
        [lax.dot_general(q, kr[0].astype(BF16), nt, preferred_element_type=F32) for kr in k_refs], axis=1)
    update(s_past, [vr[0].astype(BF16) for vr in v_refs])

    @pl.when(s_id == pl.num_programs(1) - 1)
    def _():
        s_new = lax.dot_general(q, kn_ref[0], nt, preferred_element_type=F32)
        key = lax.broadcasted_iota(jnp.int32, s_new.shape, 1)
        tok = lax.broadcasted_iota(jnp.int32, s_new.shape, 0) % ts
        s_new = jnp.where(key <= tok, s_new, NEG_BIG)
        update(s_new, [vn_ref[0]])
        lam = _diff_lambda(lq1_ref, lk1_ref, lq2_ref, lk2_ref)
        on = acc_sc[...] / l_sc[:, 0:1]
        for h in range(H_DIFF):
            cols = slice(h * DV_DIFF, (h + 1) * DV_DIFF)
            r0 = 2 * h * ts
            o = on[r0:r0 + ts, cols] - lam * on[r0 + ts:r0 + 2 * ts, cols]
            of = o * lax.rsqrt(jnp.mean(o * o, axis=-1, keepdims=True) + NORM_EPS) * (1.0 - LAM_INIT)
            o_ref[0, :, cols] = of


def _attn_sample(q_rows, k_new, v_new, cache_k, cache_v, page_table, lams, *, pages, ts):
    bs, n_pages = page_table.shape
    rows = q_rows.shape[1]
    steps = n_pages // pages
    lam_spec = pl.BlockSpec((1, DH_DIFF), lambda b, s, pt: (0, 0))

    def page_spec(i):
        return pl.BlockSpec((1, PAGE_SIZE, 512), lambda b, s, pt: (pt[b, s * pages + i], 0, 0))

    per_b = lambda r: pl.BlockSpec((1, r, 512), lambda b, s, pt: (b, 0, 0))
    grid_spec = pltpu.PrefetchScalarGridSpec(
        num_scalar_prefetch=1,
        grid=(bs, steps),
        in_specs=[per_b(rows), per_b(PAGE_SIZE), per_b(PAGE_SIZE), lam_spec, lam_spec, lam_spec, lam_spec]
                 + [page_spec(i) for i in range(pages)] + [page_spec(i) for i in range(pages)],
        out_specs=per_b(ts),
        scratch_shapes=[pltpu.VMEM((rows, V7X_LANES), F32), pltpu.VMEM((rows, V7X_LANES), F32),
                        pltpu.VMEM((rows, 512), F32)],
    )
    return pl.pallas_call(
        functools.partial(_decode_kernel, pages=pages, ts=ts),
        grid_spec=grid_spec,
        out_shape=jax.ShapeDtypeStruct((bs, ts, W_DIFF), F32),
        compiler_params=pltpu.CompilerParams(
            dimension_semantics=("arbitrary", "arbitrary"), vmem_limit_bytes=VMEM_LIMIT),
        name="attn_sample",
    )(page_table, q_rows, k_new, v_new, *lams, *([cache_k] * pages), *([cache_v] * pages))


def _retention_kernel(*refs, chunk, chunk_len, n_chunks, has_r0):
    if has_r0:
        q_ref, k_ref, v_ref, g_ref, r0_ref, o_ref, rout_ref, state = refs
    else:
        q_ref, k_ref, v_ref, g_ref, o_ref, rout_ref, state = refs
    t = pl.program_id(1)

    @pl.when(t == 0)
    def _():
        if has_r0:
            state[...] = r0_ref[0]
        else:
            state[...] = jnp.zeros(state.shape, F32)

    ii = lax.broadcasted_iota(jnp.int32, (chunk, chunk), 0)
    jj = lax.broadcasted_iota(jnp.int32, (chunk, chunk), 1)
    dif = (ii - jj).astype(F32)
    tok = lax.broadcasted_iota(jnp.int32, (chunk, DK_RET), 0).astype(F32)
    for h in range(H_RET):
        cols = slice(h * DK_RET, (h + 1) * DK_RET)
        decay = jnp.where(dif >= 0, jnp.exp(jnp.maximum(dif, 0.0) * LOG_G[h]), 0.0)
        q_dec = jnp.exp((tok + 1.0) * LOG_G[h])
        k_dec = jnp.exp((chunk_len - 1.0 - tok) * LOG_G[h])
        chunk_dec = math.exp(chunk_len * LOG_G[h])
        for c in range(n_chunks):
            rows = slice(c * chunk, (c + 1) * chunk)
            q, k, v = q_ref[0, rows, cols], k_ref[0, rows, cols], v_ref[0, rows, cols]
            r_prev = state[h]
            s = lax.dot_general(q, k, (((1,), (1,)), ((), ())), preferred_element_type=F32) * decay
            o = (jnp.dot(s.astype(q.dtype), v, preferred_element_type=F32)
                 + jnp.dot(q, r_prev.astype(q.dtype), preferred_element_type=F32) * q_dec)
            vd = (v.astype(F32) * k_dec).astype(q.dtype)
            kv = lax.dot_general(k, vd, (((0,), (0,)), ((), ())), preferred_element_type=F32)
            state[h] = r_prev * chunk_dec + kv
            mu = jnp.mean(o, axis=-1, keepdims=True)
            var = jnp.mean(jnp.square(o - mu), axis=-1, keepdims=True)
            nrm = (o - mu) * lax.rsqrt(var + GN_EPS)
            o_ref[0, rows, cols] = (nrm * _silu(g_ref[0, rows, cols].astype(F32))).astype(o_ref.dtype)

    @pl.when(t == pl.num_programs(1) - 1)
    def _():
        rout_ref[0] = state[...]


def _retention(qr, kr, vr, gr, r0, *, chunk, chunk_len, n_chunks):
    b, t, _ = qr.shape
    tr = chunk * n_chunks
    has_r0 = r0 is not None
    tok_spec = pl.BlockSpec((1, tr, 512), lambda b_, i: (b_, i, 0))
    st_spec = pl.BlockSpec((1, H_RET, DK_RET, DV_RET), lambda b_, i: (b_, 0, 0, 0))
    args = [qr, kr, vr, gr] + ([r0] if has_r0 else [])
    return pl.pallas_call(
        functools.partial(_retention_kernel, chunk=chunk, chunk_len=chunk_len, n_chunks=n_chunks,
                          has_r0=has_r0),
        grid=(b, t // tr),
        in_specs=[tok_spec] * 4 + ([st_spec] if has_r0 else []),
        out_specs=[tok_spec, st_spec],
        out_shape=[jax.ShapeDtypeStruct((b, t, W_RET), BF16),
                   jax.ShapeDtypeStruct((b, H_RET, DK_RET, DV_RET), F32)],
        scratch_shapes=[pltpu.VMEM((H_RET, DK_RET, DV_RET), F32)],
        compiler_params=pltpu.CompilerParams(
            dimension_semantics=("arbitrary", "arbitrary"), vmem_limit_bytes=VMEM_LIMIT),
        name="retention",
    )(*args)


def _gates(z):
    lane = lax.broadcasted_iota(jnp.int32, z.shape, 1)
    big = jnp.int32(1 << 20)
    neg = -jnp.inf
    is_g = lane < N_GROUPS
    glog = jnp.where(is_g, z, neg)
    gmax = jnp.max(glog, axis=-1, keepdims=True)
    gsel = jnp.min(jnp.where(glog == gmax, lane, big), axis=-1, keepdims=True)
    psum = jnp.sum(jnp.where(is_g, jnp.exp(z - gmax), 0.0), axis=-1, keepdims=True)
    pg_sel = 1.0 / psum
    e_id = lane - ROUTER_E0
    in_group = (e_id >= 0) & (e_id < N_EXPERTS) & (jnp.right_shift(e_id, 2) == gsel)
    elog = jnp.where(in_group, z, neg)
    v1 = jnp.max(elog, axis=-1, keepdims=True)
    i1 = jnp.min(jnp.where(elog == v1, lane, big), axis=-1, keepdims=True)
    elog2 = jnp.where(lane == i1, neg, elog)
    v2 = jnp.max(elog2, axis=-1, keepdims=True)
    i2 = jnp.min(jnp.where(elog2 == v2, lane, big), axis=-1, keepdims=True)
    e2 = jnp.exp(v2 - v1)
    den = 1.0 + e2
    w1 = (1.0 / den) * pg_sel
    w2 = (e2 / den) * pg_sel
    return jnp.where(lane == i1, w1, jnp.where(lane == i2, w2, 0.0))


def _ffn_kernel(x_ref, od_ref, rf_ref, beta_ref, wout_ref, gt1_ref, gffn_ref, sc2_ref, sh2_ref, gt2_ref,
                wr_ref, br_ref, wg_ref, wu_ref, wd_ref, gfin_ref, y_ref, hh_sc):
    beta = beta_ref[...]
    yd = (od_ref[0].astype(F32) * beta[:, :W_DIFF]).astype(BF16)
    yr = (rf_ref[0].astype(F32) * beta[:, W_DIFF:]).astype(BF16)
    mix = (jnp.dot(yd, wout_ref[:W_DIFF, :], preferred_element_type=F32)
           + jnp.dot(yr, wout_ref[W_DIFF:, :], preferred_element_type=F32))
    x1 = x_ref[0] + gt1_ref[0] * mix
    n2 = x1 * lax.rsqrt(jnp.mean(x1 * x1, axis=-1, keepdims=True) + NORM_EPS)
    h2 = (n2 * gffn_ref[...]) * (1.0 + sc2_ref[0]) + sh2_ref[0]
    z = jnp.dot(h2, wr_ref[...], preferred_element_type=F32, precision=lax.Precision.HIGHEST) + br_ref[...]
    gates = _gates(z)
    h2b = h2.astype(BF16)
    gw = EXPERTS_PER_GROUP * D_EXPERT
    for c in range(N_GROUPS):
        a = jnp.dot(h2b, wg_ref[:, c * gw:(c + 1) * gw], preferred_element_type=F32)
        u = jnp.dot(h2b, wu_ref[:, c * gw:(c + 1) * gw], preferred_element_type=F32)
        for e4 in range(EXPERTS_PER_GROUP):
            e = c * EXPERTS_PER_GROUP + e4
            ecols = slice(e4 * D_EXPERT, (e4 + 1) * D_EXPERT)
            gcol = gates[:, ROUTER_E0 + e:ROUTER_E0 + e + 1]
            hh = _silu(a[:, ecols]) * u[:, ecols] * gcol
            hh_sc[:, e * D_EXPERT:(e + 1) * D_EXPERT] = hh.astype(BF16)
    moe = jnp.dot(hh_sc[...], wd_ref[...], preferred_element_type=F32)
    x2 = x1 + gt2_ref[0] * moe
    y_ref[0] = (x2 * lax.rsqrt(jnp.mean(x2 * x2, axis=-1, keepdims=True) + NORM_EPS)) * gfin_ref[...]


def _ffn(x, od, rf, mods, weights, *, tm):
    bx, tx, d = x.shape
    gt1, sc2, sh2, gt2 = mods
    beta, w_out_b, g_ffn, w_router, b_router, wg_b, wu_b, wd_b, g_fin = weights
    r = gt1.shape[1]
    mod_spec = pl.BlockSpec((1, r, d), (lambda b, t: (b, 0, 0)) if r == 1 else (lambda b, t: (b, t, 0)))
    const = lambda shape: pl.BlockSpec(shape, lambda b, t: (0,) * len(shape), pipeline_mode=pl.Buffered(1))
    tok = lambda w: pl.BlockSpec((1, tm, w), lambda b, t: (b, t, 0))
    ne = N_EXPERTS * D_EXPERT
    return pl.pallas_call(
        _ffn_kernel,
        grid=(bx, tx // tm),
        in_specs=[tok(d), tok(W_DIFF), tok(W_RET), const((1, d)), const((d, d)), mod_spec, const((1, d)),
                  mod_spec, mod_spec, mod_spec, const((d, V7X_LANES)), const((1, V7X_LANES)),
                  const((d, ne)), const((d, ne)), const((ne, d)), const((1, d))],
        out_specs=tok(d),
        out_shape=jax.ShapeDtypeStruct((bx, tx, d), F32),
        scratch_shapes=[pltpu.VMEM((tm, ne), BF16)],
        compiler_params=pltpu.CompilerParams(
            dimension_semantics=("arbitrary", "arbitrary"), vmem_limit_bytes=VMEM_LIMIT),
        name="ffn",
    )(x, od, rf, beta, w_out_b, gt1, g_ffn, sc2, sh2, gt2, w_router, b_router, wg_b, wu_b, wd_b, g_fin)


def kernel(x_prompt, x_sample, cache_k, cache_v, state_ret, page_table, c_prompt, c_sample, w_ada, b_ada,
           norm_mix_g, norm_ffn_g, w_in, lambda_q1, lambda_k1, lambda_q2, lambda_k2, beta_mix, w_out,
           w_group, b_group, w_expert_router, b_expert_router, w_gate_e, w_up_e, w_down_e, final_g):
    assert w_ada.shape[0] == 1, "single-layer stack only"
    bp, tp, d = x_prompt.shape
    bs, ts, _ = x_sample.shape
    n_pages = page_table.shape[1]
    past_len = n_pages * PAGE_SIZE
    n_phys = cache_k.shape[1]

    w_in_b = w_in[0].astype(BF16)
    w_out_b = w_out[0].astype(BF16)
    ne = N_EXPERTS * D_EXPERT
    wg_b = jnp.transpose(w_gate_e[0], (1, 0, 2)).reshape(d, ne).astype(BF16)
    wu_b = jnp.transpose(w_up_e[0], (1, 0, 2)).reshape(d, ne).astype(BF16)
    wd_b = w_down_e[0].reshape(ne, d).astype(BF16)
    w_er = jnp.transpose(w_expert_router[0], (1, 0, 2)).reshape(d, N_EXPERTS)
    pad = V7X_LANES - N_GROUPS - N_EXPERTS
    w_router = jnp.concatenate([w_group[0], w_er, jnp.zeros((d, pad), F32)], axis=1)
    b_router = jnp.concatenate([b_group[0], b_expert_router[0].reshape(N_EXPERTS), jnp.zeros((pad,), F32)])[None]
    ffn_weights = (beta_mix, w_out_b, norm_ffn_g, w_router, b_router, wg_b, wu_b, wd_b, final_g[None])
    lams = (lambda_q1, lambda_k1, lambda_q2, lambda_k2)

    mod = _adaln(jnp.concatenate([c_prompt, c_sample], axis=0), w_ada[0], b_ada[0])
    mp = [m[:, None, :] for m in jnp.split(mod[:bp], 6, axis=-1)]
    ms = [jnp.repeat(m, ts, axis=0)[None] for m in jnp.split(mod[bp:], 6, axis=-1)]

    tabs_p = _rope_tables(jnp.arange(tp, dtype=F32))
    kp, vp, qT, kb, vT, qr, kr, vr, gr = _inproj(
        x_prompt, norm_mix_g, mp[1], mp[0], w_in_b, tabs_p, tm=512, transposed=True)
    od_p = _attn_prompt(qT, kb, vT, lams, tq=256, tk=256)
    rf_p, ret_p = _retention(qr, kr, vr, gr, None, chunk=RET_CHUNK, chunk_len=RET_CHUNK, n_chunks=4)
    y_p = _ffn(x_prompt, od_p, rf_p, (mp[2], mp[4], mp[3], mp[5]), ffn_weights, tm=256)

    rows_s = bs * ts
    tabs_s = _rope_tables(jnp.tile(past_len + jnp.arange(ts, dtype=F32), bs))
    ks, vs, q_s, kb_s, vb_s, qr_s, kr_s, vr_s, gr_s = _inproj(
        x_sample.reshape(1, rows_s, d), norm_mix_g, ms[1], ms[0], w_in_b, tabs_s, tm=rows_s, transposed=False)
    sub = jnp.arange(2 * H_DIFF)
    col_sub = jnp.arange(512) // DH_DIFF
    q_rows = jnp.where((sub[:, None, None] == col_sub[None, None, :])[None],
                       q_s.reshape(bs, 1, ts, 512), jnp.zeros((), BF16)).reshape(bs, 2 * H_DIFF * ts, 512)
    pad_keys = lambda a: jnp.pad(a.reshape(bs, ts, 512), ((0, 0), (0, PAGE_SIZE - ts), (0, 0)))
    od_s = _attn_sample(q_rows, pad_keys(kb_s), pad_keys(vb_s),
                        cache_k.reshape(n_phys, PAGE_SIZE, 512), cache_v.reshape(n_phys, PAGE_SIZE, 512),
                        page_table, lams, pages=8, ts=ts)
    chunk_s = 16
    pad_tok = lambda a: jnp.pad(a.reshape(bs, ts, 512), ((0, 0), (0, chunk_s - ts), (0, 0)))
    rf_s, ret_s = _retention(pad_tok(qr_s), pad_tok(kr_s), pad_tok(vr_s), pad_tok(gr_s), state_ret[0],
                             chunk=chunk_s, chunk_len=ts, n_chunks=1)
    y_s = _ffn(x_sample.reshape(1, rows_s, d), od_s.reshape(1, rows_s, 512),
               rf_s[:, :ts].reshape(1, rows_s, 512), (ms[2], ms[4], ms[3], ms[5]), ffn_weights, tm=rows_s)

    return (y_p, y_s.reshape(bs, ts, d),
            kp.reshape(1, bp, tp, 2 * H_DIFF, DH_DIFF), vp.reshape(1, bp, tp, H_DIFF, DV_DIFF), ret_p[None],
            ks.reshape(1, bs, ts, 2 * H_DIFF, DH_DIFF), vs.reshape(1, bs, ts, H_DIFF, DV_DIFF), ret_s[None])
```

```python
import functools
import math

import jax
import jax.numpy as jnp
import numpy as np
from jax import lax
from jax.experimental import pallas as pl
from jax.experimental.pallas import tpu as pltpu

F32 = jnp.float32
BF16 = jnp.bfloat16

V7X_LANES = 128
V7X_VMEM_BYTES = 64 * 1024 * 1024
VMEM_LIMIT = 56 * 1024 * 1024

D_MODEL = 1024
PAGE_SIZE = 128
H_DIFF = 4
DH_DIFF = 64
DV_DIFF = 128
W_DIFF = H_DIFF * DV_DIFF
H_RET = 4
DK_RET = 128
DV_RET = 128
W_RET = H_RET * DV_RET
IN_COLS = 7 * 512
RET_CHUNK = 128
ROPE_THETA = 10000.0
N_GROUPS = 4
EXPERTS_PER_GROUP = 4
N_EXPERTS = 16
D_EXPERT = 256
NORM_EPS = 1e-6
GN_EPS = 1e-5
LAM_INIT = 0.8 - 0.6 * math.exp(-0.3 * 0)
LOG2E = 1.4426950408889634
NEG_BIG = -1e30
LOG_G = tuple(math.log1p(-2.0 ** (-5.0 - h)) for h in range(H_RET))
ROUTER_E0 = N_GROUPS


def _silu(x):
    return x * jax.nn.sigmoid(x)


def _adaln_kernel(c_ref, w_ref, b_ref, o_ref):
    s = _silu(c_ref[...])
    o_ref[...] = jnp.dot(s, w_ref[...], preferred_element_type=F32,
                         precision=lax.Precision.HIGHEST) + b_ref[...]


def _adaln(c, w_ada, b_ada):
    n, d = c.shape
    cols = w_ada.shape[1]
    bn = 1024
    return pl.pallas_call(
        _adaln_kernel,
        grid=(cols // bn,),
        in_specs=[pl.BlockSpec((n, d), lambda j: (0, 0)),
                  pl.BlockSpec((d, bn), lambda j: (0, j)),
                  pl.BlockSpec((1, bn), lambda j: (0, j))],
        out_specs=pl.BlockSpec((n, bn), lambda j: (0, j)),
        out_shape=jax.ShapeDtypeStruct((n, cols), F32),
        name="adaln",
    )(c, w_ada, b_ada.reshape(1, cols))


def _rope64(z, cos, sin_signed, first_half):
    rot = jnp.where(first_half, pltpu.roll(z, 96, 1), pltpu.roll(z, 32, 1))
    return z * cos + rot * sin_signed


def _rope128(z, cos, sin_signed):
    return z * cos + pltpu.roll(z, 64, 1) * sin_signed


def _inproj_kernel(x_ref, g_ref, sc_ref, sh_ref, w_ref, cd_ref, sd_ref, cr_ref, sr_ref,
                   kout_ref, vout_ref, q_ref, kb_ref, v_ref, qr_ref, kr_ref, vr_ref, gr_ref,
                   *, transposed):
    x = x_ref[0]
    y = x * lax.rsqrt(jnp.mean(x * x, axis=-1, keepdims=True) + NORM_EPS)
    h = (y * g_ref[...]) * (1.0 + sc_ref[0]) + sh_ref[0]
    hb = h.astype(BF16)
    tm = x.shape[0]

    def proj(section):
        return jnp.dot(hb, w_ref[:, section * 512:(section + 1) * 512], preferred_element_type=F32)

    zq, zk, zv, zqr, zkr, zvr, zgr = [proj(i) for i in range(7)]

    cd, sd, cr, sr = cd_ref[...], sd_ref[...], cr_ref[...], sr_ref[...]
    lane = lax.broadcasted_iota(jnp.int32, (tm, V7X_LANES), 1)
    first_half = (lane % DH_DIFF) < (DH_DIFF // 2)
    q_scale = (DH_DIFF ** -0.5) * LOG2E
    for j in range(4):
        cols = slice(j * V7X_LANES, (j + 1) * V7X_LANES)
        q = _rope64(zq[:, cols], cd, sd, first_half) * q_scale
        k = _rope64(zk[:, cols], cd, sd, first_half)
        v = zv[:, cols]
        kout_ref[0, :, cols] = k
        vout_ref[0, :, cols] = v
        kb_ref[0, :, cols] = k.astype(BF16)
        if transposed:
            q_ref[0, cols, :] = q.T.astype(BF16)
            v_ref[0, cols, :] = v.T.astype(BF16)
        else:
            q_ref[0, :, cols] = q.astype(BF16)
            v_ref[0, :, cols] = v.astype(BF16)
        qr_ref[0, :, cols] = _rope128(zqr[:, cols], cr, sr).astype(BF16)
        kr_ref[0, :, cols] = (_rope128(zkr[:, cols], cr, sr) * (DK_RET ** -0.5)).astype(BF16)
        vr_ref[0, :, cols] = zvr[:, cols].astype(BF16)
        gr_ref[0, :, cols] = zgr[:, cols].astype(BF16)


def _inproj(x, g, sc, sh, w_in_b, tables, *, tm, transposed):
    bx, tx, d = x.shape
    r = sc.shape[1]
    nt = tx // tm
    mod_spec = pl.BlockSpec((1, r, d), (lambda b, t: (b, 0, 0)) if r == 1 else (lambda b, t: (b, t, 0)))
    tab_spec = pl.BlockSpec((tm, V7X_LANES), lambda b, t: (t, 0))
    nat = lambda dt: jax.ShapeDtypeStruct((bx, tx, 512), dt)
    nat_spec = pl.BlockSpec((1, tm, 512), lambda b, t: (b, t, 0))
    if transposed:
        tr = jax.ShapeDtypeStruct((bx, 512, tx), BF16)
        tr_spec = pl.BlockSpec((1, 512, tm), lambda b, t: (b, 0, t))
    else:
        tr, tr_spec = nat(BF16), nat_spec
    out_shape = [nat(F32), nat(F32), tr, nat(BF16), tr, nat(BF16), nat(BF16), nat(BF16), nat(BF16)]
    out_specs = [nat_spec, nat_spec, tr_spec, nat_spec, tr_spec, nat_spec, nat_spec, nat_spec, nat_spec]
    return pl.pallas_call(
        functools.partial(_inproj_kernel, transposed=transposed),
        grid=(bx, nt),
        in_specs=[pl.BlockSpec((1, tm, d), lambda b, t: (b, t, 0)),
                  pl.BlockSpec((1, d), lambda b, t: (0, 0)),
                  mod_spec, mod_spec,
                  pl.BlockSpec((d, IN_COLS), lambda b, t: (0, 0)),
                  tab_spec, tab_spec, tab_spec, tab_spec],
        out_specs=out_specs,
        out_shape=out_shape,
        compiler_params=pltpu.CompilerParams(
            dimension_semantics=("arbitrary", "arbitrary"), vmem_limit_bytes=VMEM_LIMIT),
        name="inproj",
    )(x, g, sc, sh, w_in_b, *tables)


def _rope_tables(pos):
    lane = np.arange(V7X_LANES)

    def one(d):
        half = d // 2
        inv = jnp.power(ROPE_THETA, -(2.0 / d) * jnp.arange(half, dtype=F32))
        ang = pos[:, None] * inv[None, :]
        idx = lane % half
        sign = jnp.asarray(np.where((lane % d) < half, -1.0, 1.0), F32)
        return jnp.cos(ang)[:, idx], jnp.sin(ang)[:, idx] * sign[None, :]

    cd, sd = one(DH_DIFF)
    cr, sr = one(DK_RET)
    return cd, sd, cr, sr


def _diff_lambda(lq1_ref, lk1_ref, lq2_ref, lk2_ref):
    e1 = jnp.exp(jnp.sum(lq1_ref[...] * lk1_ref[...], axis=-1, keepdims=True))
    e2 = jnp.exp(jnp.sum(lq2_ref[...] * lk2_ref[...], axis=-1, keepdims=True))
    return e1 - e2 + LAM_INIT


ATTN_SW = 256
ATTN_KB = 128
ATTN_TQ = 4 * ATTN_KB


def _attn_kernel(qT_ref, k_ref, vT_ref, lq1_ref, lk1_ref, lq2_ref, lk2_ref, o_ref,
                 m_sc, l_sc, acc_sc, s_buf, p_buf, a_buf, *, tq):
    sw, kb = ATTN_SW, ATTN_KB
    step_keys = 2 * kb
    assert tq == 2 * step_keys and tq % sw == 0
    ns = tq // sw
    qi = pl.program_id(2)
    qT = qT_ref[0]
    row = lax.broadcasted_iota(jnp.int32, qT.shape, 0)
    zero = jnp.zeros_like(qT)
    q_sub = (jnp.where(row < DH_DIFF, qT, zero), jnp.where(row >= DH_DIFF, qT, zero))
    m_sc[...] = jnp.full(m_sc.shape, NEG_BIG, F32)
    l_sc[...] = jnp.zeros(l_sc.shape, F32)
    acc_sc[...] = jnp.zeros(acc_sc.shape, F32)
    chains = [(a, st, par) for par in range(2) for a in range(2) for st in range(ns)]
    chain_id = lambda a, st, par: (a * ns + st) * 2 + par

    def scores(slot, key_base, diag_step):
        for a, st, par in chains:
            key0 = pl.multiple_of(key_base + par * kb, kb)
            kblk = k_ref[0, pl.ds(key0, kb), :]
            s = jnp.dot(kblk, q_sub[a][:, st * sw:(st + 1) * sw], preferred_element_type=F32)
            if diag_step is not None:
                off = diag_step * step_keys + par * kb - st * sw
                if off + kb - 1 > 0:
                    kpos = off + lax.broadcasted_iota(jnp.int32, (kb, sw), 0)
                    qpos = lax.broadcasted_iota(jnp.int32, (kb, sw), 1)
                    s = jnp.where(kpos <= qpos, s, -jnp.inf)
            s_buf[slot, chain_id(a, st, par)] = s

    def softmax(slot):
        for a, st, par in chains:
            c = chain_id(a, st, par)
            s = s_buf[slot, c]
            m_old = m_sc[c]
            m_new = jnp.maximum(m_old, jnp.max(s, axis=0, keepdims=True))
            alpha = jnp.exp2(m_old - m_new)
            p = jnp.exp2(s - m_new)
            l_sc[c] = alpha * l_sc[c] + jnp.sum(p, axis=0, keepdims=True)
            m_sc[c] = m_new
            a_buf[slot, c] = alpha
            p_buf[slot, c] = p.astype(BF16)

    def values(slot, key_base):
        for a, st, par in chains:
            c = chain_id(a, st, par)
            key0 = pl.multiple_of(key_base + par * kb, kb)
            vblk = vT_ref[0, :, pl.ds(key0, kb)]
            acc_sc[c] = acc_sc[c] * a_buf[slot, c] + jnp.dot(vblk, p_buf[slot, c], preferred_element_type=F32)

    diag_base = [qi * tq + d * step_keys for d in range(2)]
    scores(0, diag_base[0], 0)
    scores(1, diag_base[1], 1)
    softmax(0)

    def trip_pair(u, carry):
        for par_t in range(2):
            full_base = (2 * u + par_t) * step_keys
            lag_base = jnp.where(u == 0, diag_base[par_t], full_base - 2 * step_keys)
            scores(par_t, full_base, None)
            softmax(1 - par_t)
            values(par_t, lag_base)
        return carry

    lax.fori_loop(0, qi, trip_pair, 0)
    last = [jnp.where(qi == 0, diag_base[d], (2 * qi - 2 + d) * step_keys) for d in range(2)]
    softmax(1)
    values(0, last[0])
    values(1, last[1])

    lam = _diff_lambda(lq1_ref, lk1_ref, lq2_ref, lk2_ref)
    for st in range(ns):
        outs = []
        for a in range(2):
            c0, c1 = chain_id(a, st, 0), chain_id(a, st, 1)
            m = jnp.maximum(m_sc[c0], m_sc[c1])
            w0, w1 = jnp.exp2(m_sc[c0] - m), jnp.exp2(m_sc[c1] - m)
            l = l_sc[c0] * w0 + l_sc[c1] * w1
            outs.append((acc_sc[c0] * w0 + acc_sc[c1] * w1) / l)
        o = outs[0] - lam * outs[1]
        of = o * lax.rsqrt(jnp.mean(o * o, axis=0, keepdims=True) + NORM_EPS) * (1.0 - LAM_INIT)
        o_ref[0, st * sw:(st + 1) * sw, :] = of.T.astype(o_ref.dtype)


def _attn_prompt(qT, kb, vT, lams, *, tq):
    b, _, t = qT.shape
    lam_spec = pl.BlockSpec((1, DH_DIFF), lambda b_, h, i: (0, 0))
    n_chains = 2 * (tq // ATTN_SW) * 2
    return pl.pallas_call(
        functools.partial(_attn_kernel, tq=tq),
        grid=(b, H_DIFF, t // tq),
        in_specs=[pl.BlockSpec((1, V7X_LANES, tq), lambda b_, h, i: (b_, h, i)),
                  pl.BlockSpec((1, t, V7X_LANES), lambda b_, h, i: (b_, 0, h)),
                  pl.BlockSpec((1, V7X_LANES, t), lambda b_, h, i: (b_, h, 0)),
                  lam_spec, lam_spec, lam_spec, lam_spec],
        out_specs=pl.BlockSpec((1, tq, V7X_LANES), lambda b_, h, i: (b_, i, h)),
        out_shape=jax.ShapeDtypeStruct((b, t, W_DIFF), BF16),
        scratch_shapes=[pltpu.VMEM((n_chains, 1, ATTN_SW), F32), pltpu.VMEM((n_chains, 1, ATTN_SW), F32),
                        pltpu.VMEM((n_chains, DV_DIFF, ATTN_SW), F32),
                        pltpu.VMEM((2, n_chains, ATTN_KB, ATTN_SW), F32),
                        pltpu.VMEM((2, n_chains, ATTN_KB, ATTN_SW), BF16),
                        pltpu.VMEM((2, n_chains, 1, ATTN_SW), F32)],
        compiler_params=pltpu.CompilerParams(
            dimension_semantics=("arbitrary", "arbitrary", "arbitrary"), vmem_limit_bytes=VMEM_LIMIT),
        name="attn_prompt",
    )(qT, kb, vT, *lams)


def _decode_kernel(pt_ref, q_ref, kn_ref, vn_ref, lq1_ref, lk1_ref, lq2_ref, lk2_ref, *rest, pages, ts):
    k_refs, v_refs = rest[:pages], rest[pages:2 * pages]
    o_ref, m_sc, l_sc, acc_sc = rest[2 * pages:]
    s_id = pl.program_id(1)
    q = q_ref[0]
    hr = 2 * ts

    @pl.when(s_id == 0)
    def _():
        m_sc[...] = jnp.full(m_sc.shape, NEG_BIG, F32)
        l_sc[...] = jnp.zeros(l_sc.shape, F32)
        acc_sc[...] = jnp.zeros(acc_sc.shape, F32)

    def update(s, v_heads):
        m_old = m_sc[:, 0:1]
        m_new = jnp.maximum(m_old, jnp.max(s, axis=1, keepdims=True))
        alpha = jnp.exp2(m_old - m_new)
        p = jnp.exp2(s - m_new)
        l_new = alpha * l_sc[:, 0:1] + jnp.sum(p, axis=1, keepdims=True)
        pb = p.astype(BF16)
        for h in range(H_DIFF):
            rows = slice(h * hr, (h + 1) * hr)
            pv = jnp.dot(pb[rows, :], v_heads[h], preferred_element_type=F32)
            acc_sc[rows, :] = acc_sc[rows, :] * alpha[rows, :] + pv
        m_sc[...] = jnp.broadcast_to(m_new, m_sc.shape)
        l_sc[...] = jnp.broadcast_to(l_new, l_sc.shape)

    kT = jnp.concatenate([kr[0].astype(BF16) for kr in k_refs], axis=1)
    s_past = jnp.dot(q, kT, preferred_element_type=F32)
    update(s_past, [jnp.concatenate([vr[0, pl.ds(h, PAGE_SIZE, stride=H_DIFF), :].astype(BF16)
                                     for vr in v_refs], axis=0) for h in range(H_DIFF)])

    @pl.when(s_id == pl.num_programs(1) - 1)
    def _():
        s_new = lax.dot_general(q, kn_ref[0], (((1,), (1,)), ((), ())), preferred_element_type=F32)
        key = lax.broadcasted_iota(jnp.int32, s_new.shape, 1)
        tok = lax.broadcasted_iota(jnp.int32, s_new.shape, 0) % ts
        s_new = jnp.where(key <= tok, s_new, NEG_BIG)
        update(s_new, [vn_ref[0, :, h * DV_DIFF:(h + 1) * DV_DIFF] for h in range(H_DIFF)])
        lam = _diff_lambda(lq1_ref, lk1_ref, lq2_ref, lk2_ref)
        on = acc_sc[...] / l_sc[:, 0:1]
        for h in range(H_DIFF):
            o = on[h * hr:h * hr + ts, :] - lam * on[h * hr + ts:(h + 1) * hr, :]
            of = o * lax.rsqrt(jnp.mean(o * o, axis=-1, keepdims=True) + NORM_EPS) * (1.0 - LAM_INIT)
            o_ref[0, :, h * DV_DIFF:(h + 1) * DV_DIFF] = of


def _attn_sample(q_rows, k_new, v_new, cache_k, cache_v, page_table, lams, *, pages, ts):
    bs, n_pages = page_table.shape
    rows = q_rows.shape[1]
    steps = n_pages // pages
    lam_spec = pl.BlockSpec((1, DH_DIFF), lambda b, s, pt: (0, 0))

    assert 2 * ts == 8, "one head's query rows must fill one sublane tile"

    def page_spec(i):
        return pl.BlockSpec((1, 512, V7X_LANES), lambda b, s, pt: (pt[b, s * pages + i], 0, 0))

    per_b = lambda r: pl.BlockSpec((1, r, 512), lambda b, s, pt: (b, 0, 0))
    grid_spec = pltpu.PrefetchScalarGridSpec(
        num_scalar_prefetch=1,
        grid=(bs, steps),
        in_specs=[per_b(rows), per_b(PAGE_SIZE), per_b(PAGE_SIZE), lam_spec, lam_spec, lam_spec, lam_spec]
                 + [page_spec(i) for i in range(pages)] + [page_spec(i) for i in range(pages)],
        out_specs=per_b(ts),
        scratch_shapes=[pltpu.VMEM((rows, V7X_LANES), F32), pltpu.VMEM((rows, V7X_LANES), F32),
                        pltpu.VMEM((rows, DV_DIFF), F32)],
    )
    return pl.pallas_call(
        functools.partial(_decode_kernel, pages=pages, ts=ts),
        grid_spec=grid_spec,
        out_shape=jax.ShapeDtypeStruct((bs, ts, W_DIFF), F32),
        compiler_params=pltpu.CompilerParams(
            dimension_semantics=("arbitrary", "arbitrary"), vmem_limit_bytes=VMEM_LIMIT),
        name="attn_sample",
    )(page_table, q_rows, k_new, v_new, *lams, *([cache_k] * pages), *([cache_v] * pages))


def _retention_kernel(*refs, chunk, chunk_len, n_chunks, has_r0):
    if has_r0:
        q_ref, k_ref, v_ref, g_ref, r0_ref, o_ref, rout_ref, state = refs
    else:
        q_ref, k_ref, v_ref, g_ref, o_ref, rout_ref, state = refs
    t = pl.program_id(1)

    @pl.when(t == 0)
    def _():
        if has_r0:
            state[...] = r0_ref[0]
        else:
            state[...] = jnp.zeros(state.shape, F32)

    ii = lax.broadcasted_iota(jnp.int32, (chunk, chunk), 0)
    jj = lax.broadcasted_iota(jnp.int32, (chunk, chunk), 1)
    dif = (ii - jj).astype(F32)
    tok = lax.broadcasted_iota(jnp.int32, (chunk, DK_RET), 0).astype(F32)
    for h in range(H_RET):
        cols = slice(h * DK_RET, (h + 1) * DK_RET)
        decay = jnp.where(dif >= 0, jnp.exp(jnp.maximum(dif, 0.0) * LOG_G[h]), 0.0)
        q_dec = jnp.exp((tok + 1.0) * LOG_G[h])
        k_dec = jnp.exp((chunk_len - 1.0 - tok) * LOG_G[h])
        chunk_dec = math.exp(chunk_len * LOG_G[h])
        for c in range(n_chunks):
            rows = slice(c * chunk, (c + 1) * chunk)
            q, k, v = q_ref[0, rows, cols], k_ref[0, rows, cols], v_ref[0, rows, cols]
            r_prev = state[h]
            s = lax.dot_general(q, k, (((1,), (1,)), ((), ())), preferred_element_type=F32) * decay
            o = (jnp.dot(s.astype(q.dtype), v, preferred_element_type=F32)
                 + jnp.dot(q, r_prev.astype(q.dtype), preferred_element_type=F32) * q_dec)
            vd = (v.astype(F32) * k_dec).astype(q.dtype)
            kv = lax.dot_general(k, vd, (((0,), (0,)), ((), ())), preferred_element_type=F32)
            state[h] = r_prev * chunk_dec + kv
            mu = jnp.mean(o, axis=-1, keepdims=True)
            var = jnp.mean(jnp.square(o - mu), axis=-1, keepdims=True)
            nrm = (o - mu) * lax.rsqrt(var + GN_EPS)
            o_ref[0, rows, cols] = (nrm * _silu(g_ref[0, rows, cols].astype(F32))).astype(o_ref.dtype)

    @pl.when(t == pl.num_programs(1) - 1)
    def _():
        rout_ref[0] = state[...]


def _retention(qr, kr, vr, gr, r0, *, chunk, chunk_len, n_chunks):
    b, t, _ = qr.shape
    tr = chunk * n_chunks
    has_r0 = r0 is not None
    tok_spec = pl.BlockSpec((1, tr, 512), lambda b_, i: (b_, i, 0))
    st_spec = pl.BlockSpec((1, H_RET, DK_RET, DV_RET), lambda b_, i: (b_, 0, 0, 0))
    args = [qr, kr, vr, gr] + ([r0] if has_r0 else [])
    return pl.pallas_call(
        functools.partial(_retention_kernel, chunk=chunk, chunk_len=chunk_len, n_chunks=n_chunks,
                          has_r0=has_r0),
        grid=(b, t // tr),
        in_specs=[tok_spec] * 4 + ([st_spec] if has_r0 else []),
        out_specs=[tok_spec, st_spec],
        out_shape=[jax.ShapeDtypeStruct((b, t, W_RET), BF16),
                   jax.ShapeDtypeStruct((b, H_RET, DK_RET, DV_RET), F32)],
        scratch_shapes=[pltpu.VMEM((H_RET, DK_RET, DV_RET), F32)],
        compiler_params=pltpu.CompilerParams(
            dimension_semantics=("arbitrary", "arbitrary"), vmem_limit_bytes=VMEM_LIMIT),
        name="retention",
    )(*args)


def _gates(z):
    lane = lax.broadcasted_iota(jnp.int32, z.shape, 1)
    big = jnp.int32(1 << 20)
    neg = -jnp.inf
    is_g = lane < N_GROUPS
    glog = jnp.where(is_g, z, neg)
    gmax = jnp.max(glog, axis=-1, keepdims=True)
    gsel = jnp.min(jnp.where(glog == gmax, lane, big), axis=-1, keepdims=True)
    psum = jnp.sum(jnp.where(is_g, jnp.exp(z - gmax), 0.0), axis=-1, keepdims=True)
    pg_sel = 1.0 / psum
    e_id = lane - ROUTER_E0
    in_group = (e_id >= 0) & (e_id < N_EXPERTS) & (jnp.right_shift(e_id, 2) == gsel)
    elog = jnp.where(in_group, z, neg)
    v1 = jnp.max(elog, axis=-1, keepdims=True)
    i1 = jnp.min(jnp.where(elog == v1, lane, big), axis=-1, keepdims=True)
    elog2 = jnp.where(lane == i1, neg, elog)
    v2 = jnp.max(elog2, axis=-1, keepdims=True)
    i2 = jnp.min(jnp.where(elog2 == v2, lane, big), axis=-1, keepdims=True)
    e2 = jnp.exp(v2 - v1)
    den = 1.0 + e2
    w1 = (1.0 / den) * pg_sel
    w2 = (e2 / den) * pg_sel
    return jnp.where(lane == i1, w1, jnp.where(lane == i2, w2, 0.0))


def _ffn_kernel(x_ref, od_ref, rf_ref, beta_ref, wout_ref, gt1_ref, gffn_ref, sc2_ref, sh2_ref, gt2_ref,
                wr_ref, br_ref, wg_ref, wu_ref, wd_ref, gfin_ref, y_ref, hh_sc):
    beta = beta_ref[...]
    yd = (od_ref[0].astype(F32) * beta[:, :W_DIFF]).astype(BF16)
    yr = (rf_ref[0].astype(F32) * beta[:, W_DIFF:]).astype(BF16)
    mix = (jnp.dot(yd, wout_ref[:W_DIFF, :], preferred_element_type=F32)
           + jnp.dot(yr, wout_ref[W_DIFF:, :], preferred_element_type=F32))
    x1 = x_ref[0] + gt1_ref[0] * mix
    n2 = x1 * lax.rsqrt(jnp.mean(x1 * x1, axis=-1, keepdims=True) + NORM_EPS)
    h2 = (n2 * gffn_ref[...]) * (1.0 + sc2_ref[0]) + sh2_ref[0]
    z = jnp.dot(h2, wr_ref[...], preferred_element_type=F32, precision=lax.Precision.HIGHEST) + br_ref[...]
    gates = _gates(z)
    h2b = h2.astype(BF16)
    gw = EXPERTS_PER_GROUP * D_EXPERT
    for c in range(N_GROUPS):
        a = jnp.dot(h2b, wg_ref[:, c * gw:(c + 1) * gw], preferred_element_type=F32)
        u = jnp.dot(h2b, wu_ref[:, c * gw:(c + 1) * gw], preferred_element_type=F32)
        for e4 in range(EXPERTS_PER_GROUP):
            e = c * EXPERTS_PER_GROUP + e4
            ecols = slice(e4 * D_EXPERT, (e4 + 1) * D_EXPERT)
            gcol = gates[:, ROUTER_E0 + e:ROUTER_E0 + e + 1]
            hh = _silu(a[:, ecols]) * u[:, ecols] * gcol
            hh_sc[:, e * D_EXPERT:(e + 1) * D_EXPERT] = hh.astype(BF16)
    moe = jnp.dot(hh_sc[...], wd_ref[...], preferred_element_type=F32)
    x2 = x1 + gt2_ref[0] * moe
    y_ref[0] = (x2 * lax.rsqrt(jnp.mean(x2 * x2, axis=-1, keepdims=True) + NORM_EPS)) * gfin_ref[...]


def _ffn(x, od, rf, mods, weights, *, tm):
    bx, tx, d = x.shape
    gt1, sc2, sh2, gt2 = mods
    beta, w_out_b, g_ffn, w_router, b_router, wg_b, wu_b, wd_b, g_fin = weights
    r = gt1.shape[1]
    mod_spec = pl.BlockSpec((1, r, d), (lambda b, t: (b, 0, 0)) if r == 1 else (lambda b, t: (b, t, 0)))
    const = lambda shape: pl.BlockSpec(shape, lambda b, t: (0,) * len(shape), pipeline_mode=pl.Buffered(1))
    tok = lambda w: pl.BlockSpec((1, tm, w), lambda b, t: (b, t, 0))
    ne = N_EXPERTS * D_EXPERT
    return pl.pallas_call(
        _ffn_kernel,
        grid=(bx, tx // tm),
        in_specs=[tok(d), tok(W_DIFF), tok(W_RET), const((1, d)), const((d, d)), mod_spec, const((1, d)),
                  mod_spec, mod_spec, mod_spec, const((d, V7X_LANES)), const((1, V7X_LANES)),
                  const((d, ne)), const((d, ne)), const((ne, d)), const((1, d))],
        out_specs=tok(d),
        out_shape=jax.ShapeDtypeStruct((bx, tx, d), F32),
        scratch_shapes=[pltpu.VMEM((tm, ne), BF16)],
        compiler_params=pltpu.CompilerParams(
            dimension_semantics=("arbitrary", "arbitrary"), vmem_limit_bytes=VMEM_LIMIT),
        name="ffn",
    )(x, od, rf, beta, w_out_b, gt1, g_ffn, sc2, sh2, gt2, w_router, b_router, wg_b, wu_b, wd_b, g_fin)


def kernel(x_prompt, x_sample, cache_k, cache_v, state_ret, page_table, c_prompt, c_sample, w_ada, b_ada,
           norm_mix_g, norm_ffn_g, w_in, lambda_q1, lambda_k1, lambda_q2, lambda_k2, beta_mix, w_out,
           w_group, b_group, w_expert_router, b_expert_router, w_gate_e, w_up_e, w_down_e, final_g):
    assert w_ada.shape[0] == 1, "single-layer stack only"
    bp, tp, d = x_prompt.shape
    bs, ts, _ = x_sample.shape
    n_pages = page_table.shape[1]
    past_len = n_pages * PAGE_SIZE
    n_phys = cache_k.shape[1]

    w_in_b = w_in[0].astype(BF16)
    w_out_b = w_out[0].astype(BF16)
    ne = N_EXPERTS * D_EXPERT
    wg_b = jnp.transpose(w_gate_e[0], (1, 0, 2)).reshape(d, ne).astype(BF16)
    wu_b = jnp.transpose(w_up_e[0], (1, 0, 2)).reshape(d, ne).astype(BF16)
    wd_b = w_down_e[0].reshape(ne, d).astype(BF16)
    w_er = jnp.transpose(w_expert_router[0], (1, 0, 2)).reshape(d, N_EXPERTS)
    pad = V7X_LANES - N_GROUPS - N_EXPERTS
    w_router = jnp.concatenate([w_group[0], w_er, jnp.zeros((d, pad), F32)], axis=1)
    b_router = jnp.concatenate([b_group[0], b_expert_router[0].reshape(N_EXPERTS), jnp.zeros((pad,), F32)])[None]
    ffn_weights = (beta_mix, w_out_b, norm_ffn_g, w_router, b_router, wg_b, wu_b, wd_b, final_g[None])
    lams = (lambda_q1, lambda_k1, lambda_q2, lambda_k2)

    mod = _adaln(jnp.concatenate([c_prompt, c_sample], axis=0), w_ada[0], b_ada[0])
    mp = [m[:, None, :] for m in jnp.split(mod[:bp], 6, axis=-1)]
    ms = [jnp.repeat(m, ts, axis=0)[None] for m in jnp.split(mod[bp:], 6, axis=-1)]

    tabs_p = _rope_tables(jnp.arange(tp, dtype=F32))
    kp, vp, qT, kb, vT, qr, kr, vr, gr = _inproj(
        x_prompt, norm_mix_g, mp[1], mp[0], w_in_b, tabs_p, tm=512, transposed=True)
    od_p = _attn_prompt(qT, kb, vT, lams, tq=ATTN_TQ)
    rf_p, ret_p = _retention(qr, kr, vr, gr, None, chunk=RET_CHUNK, chunk_len=RET_CHUNK, n_chunks=4)
    y_p = _ffn(x_prompt, od_p, rf_p, (mp[2], mp[4], mp[3], mp[5]), ffn_weights, tm=256)

    rows_s = bs * ts
    tabs_s = _rope_tables(jnp.tile(past_len + jnp.arange(ts, dtype=F32), bs))
    ks, vs, q_s, kb_s, vb_s, qr_s, kr_s, vr_s, gr_s = _inproj(
        x_sample.reshape(1, rows_s, d), norm_mix_g, ms[1], ms[0], w_in_b, tabs_s, tm=rows_s, transposed=False)
    sub = jnp.arange(2 * H_DIFF)
    col_sub = jnp.arange(512) // DH_DIFF
    q_rows = jnp.where((sub[:, None, None] == col_sub[None, None, :])[None],
                       q_s.reshape(bs, 1, ts, 512), jnp.zeros((), BF16)).reshape(bs, 2 * H_DIFF * ts, 512)
    pad_keys = lambda a: jnp.pad(a.reshape(bs, ts, 512), ((0, 0), (0, PAGE_SIZE - ts), (0, 0)))
    ck = jnp.transpose(cache_k[0], (0, 2, 3, 1)).reshape(n_phys, 2 * H_DIFF * DH_DIFF, PAGE_SIZE)
    cv = cache_v[0].reshape(n_phys, PAGE_SIZE * H_DIFF, DV_DIFF)
    od_s = _attn_sample(q_rows, pad_keys(kb_s), pad_keys(vb_s), ck, cv, page_table, lams, pages=16, ts=ts)
    chunk_s = 16
    pad_tok = lambda a: jnp.pad(a.reshape(bs, ts, 512), ((0, 0), (0, chunk_s - ts), (0, 0)))
    rf_s, ret_s = _retention(pad_tok(qr_s), pad_tok(kr_s), pad_tok(vr_s), pad_tok(gr_s), state_ret[0],
                             chunk=chunk_s, chunk_len=ts, n_chunks=1)
    y_s = _ffn(x_sample.reshape(1, rows_s, d), od_s.reshape(1, rows_s, 512),
               rf_s[:, :ts].reshape(1, rows_s, 512), (ms[2], ms[4], ms[3], ms[5]), ffn_weights, tm=rows_s)

    return (y_p, y_s.reshape(bs, ts, d),
            kp.reshape(1, bp, tp, 2 * H_DIFF, DH_DIFF), vp.reshape(1, bp, tp, H_DIFF, DV_DIFF), ret_p[None],
            ks.reshape(1, bs, ts, 2 * H_DIFF, DH_DIFF), vs.reshape(1, bs, ts, H_DIFF, DV_DIFF), ret_s[None])
```

```python
import functools
import math

import jax
import jax.numpy as jnp
import numpy as np
from jax import lax
from jax.experimental import pallas as pl
from jax.experimental.pallas import tpu as pltpu

F32 = jnp.float32
BF16 = jnp.bfloat16

V7X_LANES = 128
V7X_VMEM_BYTES = 64 * 1024 * 1024
VMEM_LIMIT = 56 * 1024 * 1024

D_MODEL = 1024
PAGE_SIZE = 128
H_DIFF = 4
DH_DIFF = 64
DV_DIFF = 128
W_DIFF = H_DIFF * DV_DIFF
H_RET = 4
DK_RET = 128
DV_RET = 128
W_RET = H_RET * DV_RET
IN_COLS = 7 * 512
RET_CHUNK = 128
ROPE_THETA = 10000.0
N_GROUPS = 4
EXPERTS_PER_GROUP = 4
N_EXPERTS = 16
D_EXPERT = 256
NORM_EPS = 1e-6
GN_EPS = 1e-5
LAM_INIT = 0.8 - 0.6 * math.exp(-0.3 * 0)
LOG2E = 1.4426950408889634
NEG_BIG = -1e30
LOG_G = tuple(math.log1p(-2.0 ** (-5.0 - h)) for h in range(H_RET))
ROUTER_E0 = N_GROUPS


def _silu(x):
    return x * jax.nn.sigmoid(x)


def _adaln_kernel(c_ref, w_ref, b_ref, o_ref):
    s = _silu(c_ref[...])
    o_ref[...] = jnp.dot(s, w_ref[...], preferred_element_type=F32,
                         precision=lax.Precision.HIGHEST) + b_ref[...]


def _adaln(c, w_ada, b_ada):
    n, d = c.shape
    cols = w_ada.shape[1]
    bn = 1024
    return pl.pallas_call(
        _adaln_kernel,
        grid=(cols // bn,),
        in_specs=[pl.BlockSpec((n, d), lambda j: (0, 0)),
                  pl.BlockSpec((d, bn), lambda j: (0, j)),
                  pl.BlockSpec((1, bn), lambda j: (0, j))],
        out_specs=pl.BlockSpec((n, bn), lambda j: (0, j)),
        out_shape=jax.ShapeDtypeStruct((n, cols), F32),
        name="adaln",
    )(c, w_ada, b_ada.reshape(1, cols))


def _rope64(z, cos, sin_signed, first_half):
    rot = jnp.where(first_half, pltpu.roll(z, 96, 1), pltpu.roll(z, 32, 1))
    return z * cos + rot * sin_signed


def _rope128(z, cos, sin_signed):
    return z * cos + pltpu.roll(z, 64, 1) * sin_signed


def _inproj_kernel(x_ref, g_ref, sc_ref, sh_ref, w_ref, cd_ref, sd_ref, cr_ref, sr_ref,
                   kout_ref, vout_ref, q_ref, kb_ref, v_ref, qr_ref, kr_ref, vr_ref, gr_ref,
                   *, transposed):
    x = x_ref[0]
    y = x * lax.rsqrt(jnp.mean(x * x, axis=-1, keepdims=True) + NORM_EPS)
    h = (y * g_ref[...]) * (1.0 + sc_ref[0]) + sh_ref[0]
    hb = h.astype(BF16)
    tm = x.shape[0]

    def proj(section):
        return jnp.dot(hb, w_ref[:, section * 512:(section + 1) * 512], preferred_element_type=F32)

    zq, zk, zv, zqr, zkr, zvr, zgr = [proj(i) for i in range(7)]

    cd, sd, cr, sr = cd_ref[...], sd_ref[...], cr_ref[...], sr_ref[...]
    lane = lax.broadcasted_iota(jnp.int32, (tm, V7X_LANES), 1)
    first_half = (lane % DH_DIFF) < (DH_DIFF // 2)
    q_scale = (DH_DIFF ** -0.5) * LOG2E
    for j in range(4):
        cols = slice(j * V7X_LANES, (j + 1) * V7X_LANES)
        q = _rope64(zq[:, cols], cd, sd, first_half) * q_scale
        k = _rope64(zk[:, cols], cd, sd, first_half)
        v = zv[:, cols]
        kout_ref[0, :, cols] = k
        vout_ref[0, pl.ds(j, tm, stride=H_DIFF), :] = v
        kb_ref[0, :, cols] = k.astype(BF16)
        if transposed:
            q_ref[0, cols, :] = q.T.astype(BF16)
            v_ref[0, cols, :] = v.T.astype(BF16)
        else:
            q_ref[0, :, cols] = q.astype(BF16)
            v_ref[0, :, cols] = v.astype(BF16)
        qr_ref[0, :, cols] = _rope128(zqr[:, cols], cr, sr).astype(BF16)
        kr_ref[0, :, cols] = (_rope128(zkr[:, cols], cr, sr) * (DK_RET ** -0.5)).astype(BF16)
        vr_ref[0, :, cols] = zvr[:, cols].astype(BF16)
        gr_ref[0, :, cols] = zgr[:, cols].astype(BF16)


def _inproj(x, g, sc, sh, w_in_b, tables, *, tm, transposed):
    bx, tx, d = x.shape
    r = sc.shape[1]
    nt = tx // tm
    mod_spec = pl.BlockSpec((1, r, d), (lambda b, t: (b, 0, 0)) if r == 1 else (lambda b, t: (b, t, 0)))
    tab_spec = pl.BlockSpec((tm, V7X_LANES), lambda b, t: (t, 0))
    nat = lambda dt: jax.ShapeDtypeStruct((bx, tx, 512), dt)
    nat_spec = pl.BlockSpec((1, tm, 512), lambda b, t: (b, t, 0))
    if transposed:
        tr = jax.ShapeDtypeStruct((bx, 512, tx), BF16)
        tr_spec = pl.BlockSpec((1, 512, tm), lambda b, t: (b, 0, t))
    else:
        tr, tr_spec = nat(BF16), nat_spec
    vrows = jax.ShapeDtypeStruct((bx, tx * H_DIFF, DV_DIFF), F32)
    vrows_spec = pl.BlockSpec((1, tm * H_DIFF, DV_DIFF), lambda b, t: (b, t, 0))
    out_shape = [nat(F32), vrows, tr, nat(BF16), tr, nat(BF16), nat(BF16), nat(BF16), nat(BF16)]
    out_specs = [nat_spec, vrows_spec, tr_spec, nat_spec, tr_spec, nat_spec, nat_spec, nat_spec, nat_spec]
    return pl.pallas_call(
        functools.partial(_inproj_kernel, transposed=transposed),
        grid=(bx, nt),
        in_specs=[pl.BlockSpec((1, tm, d), lambda b, t: (b, t, 0)),
                  pl.BlockSpec((1, d), lambda b, t: (0, 0)),
                  mod_spec, mod_spec,
                  pl.BlockSpec((d, IN_COLS), lambda b, t: (0, 0)),
                  tab_spec, tab_spec, tab_spec, tab_spec],
        out_specs=out_specs,
        out_shape=out_shape,
        compiler_params=pltpu.CompilerParams(
            dimension_semantics=("arbitrary", "arbitrary"), vmem_limit_bytes=VMEM_LIMIT),
        name="inproj",
    )(x, g, sc, sh, w_in_b, *tables)


def _rope_tables(pos):
    lane = np.arange(V7X_LANES)

    def one(d):
        half = d // 2
        inv = jnp.power(ROPE_THETA, -(2.0 / d) * jnp.arange(half, dtype=F32))
        ang = pos[:, None] * inv[None, :]
        idx = lane % half
        sign = jnp.asarray(np.where((lane % d) < half, -1.0, 1.0), F32)
        return jnp.cos(ang)[:, idx], jnp.sin(ang)[:, idx] * sign[None, :]

    cd, sd = one(DH_DIFF)
    cr, sr = one(DK_RET)
    return cd, sd, cr, sr


def _diff_lambda(lq1_ref, lk1_ref, lq2_ref, lk2_ref):
    e1 = jnp.exp(jnp.sum(lq1_ref[...] * lk1_ref[...], axis=-1, keepdims=True))
    e2 = jnp.exp(jnp.sum(lq2_ref[...] * lk2_ref[...], axis=-1, keepdims=True))
    return e1 - e2 + LAM_INIT


ATTN_SW = 256
ATTN_KB = 256
ATTN_TQ = 2 * ATTN_KB


def _attn_kernel(qT_ref, k_ref, vT_ref, lq1_ref, lk1_ref, lq2_ref, lk2_ref, o_ref,
                 m_sc, l_sc, acc_sc, s_buf, p_buf, a_buf, *, tq):
    sw, kb = ATTN_SW, ATTN_KB
    assert tq == 2 * kb and tq % sw == 0
    ns = tq // sw
    qi = pl.program_id(2)
    qT = qT_ref[0]
    row = lax.broadcasted_iota(jnp.int32, qT.shape, 0)
    zero = jnp.zeros_like(qT)
    q_sub = (jnp.where(row < DH_DIFF, qT, zero), jnp.where(row >= DH_DIFF, qT, zero))
    m_sc[...] = jnp.full(m_sc.shape, NEG_BIG, F32)
    l_sc[...] = jnp.zeros(l_sc.shape, F32)
    acc_sc[...] = jnp.zeros(acc_sc.shape, F32)
    chains = [(a, st) for a in range(2) for st in range(ns)]
    chain_id = lambda a, st: a * ns + st

    def scores(slot, key_base, diag_step):
        kblk = k_ref[0, pl.ds(pl.multiple_of(key_base, kb), kb), :]
        for a, st in chains:
            s = jnp.dot(kblk, q_sub[a][:, st * sw:(st + 1) * sw], preferred_element_type=F32)
            if diag_step is not None:
                off = diag_step * kb - st * sw
                if off + kb - 1 > 0:
                    kpos = off + lax.broadcasted_iota(jnp.int32, (kb, sw), 0)
                    qpos = lax.broadcasted_iota(jnp.int32, (kb, sw), 1)
                    s = jnp.where(kpos <= qpos, s, -jnp.inf)
            s_buf[slot, chain_id(a, st)] = s

    def softmax(slot):
        for a, st in chains:
            c = chain_id(a, st)
            m_old = m_sc[c]
            m_new = jnp.maximum(m_old, jnp.max(s_buf[slot, c], axis=0, keepdims=True))
            alpha = jnp.exp2(m_old - m_new)
            p = jnp.exp2(s_buf[slot, c] - m_new)
            l_sc[c] = alpha * l_sc[c] + jnp.sum(p, axis=0, keepdims=True)
            m_sc[c] = m_new
            a_buf[slot, c] = alpha
            p_buf[slot, c] = p.astype(BF16)

    def values(slot, key_base):
        vblk = vT_ref[0, :, pl.ds(pl.multiple_of(key_base, kb), kb)]
        for a, st in chains:
            c = chain_id(a, st)
            acc_sc[c] = acc_sc[c] * a_buf[slot, c] + jnp.dot(vblk, p_buf[slot, c], preferred_element_type=F32)

    diag_base = [qi * tq + d * kb for d in range(2)]
    scores(0, diag_base[0], 0)
    scores(1, diag_base[1], 1)
    softmax(0)

    def trip_pair(u, carry):
        for par_t in range(2):
            full_base = (2 * u + par_t) * kb
            lag_base = jnp.where(u == 0, diag_base[par_t], full_base - 2 * kb)
            scores(par_t, full_base, None)
            softmax(1 - par_t)
            values(par_t, lag_base)
        return carry

    lax.fori_loop(0, qi, trip_pair, 0)
    last = [jnp.where(qi == 0, diag_base[d], (2 * qi - 2 + d) * kb) for d in range(2)]
    softmax(1)
    values(0, last[0])
    values(1, last[1])

    lam = _diff_lambda(lq1_ref, lk1_ref, lq2_ref, lk2_ref)
    for st in range(ns):
        outs = [acc_sc[chain_id(a, st)] / l_sc[chain_id(a, st)] for a in range(2)]
        o = outs[0] - lam * outs[1]
        of = o * lax.rsqrt(jnp.mean(o * o, axis=0, keepdims=True) + NORM_EPS) * (1.0 - LAM_INIT)
        o_ref[0, st * sw:(st + 1) * sw, :] = of.T.astype(o_ref.dtype)


def _attn_prompt(qT, kb, vT, lams, *, tq):
    b, _, t = qT.shape
    lam_spec = pl.BlockSpec((1, DH_DIFF), lambda b_, h, i: (0, 0))
    n_chains = 2 * (tq // ATTN_SW)
    return pl.pallas_call(
        functools.partial(_attn_kernel, tq=tq),
        grid=(b, H_DIFF, t // tq),
        in_specs=[pl.BlockSpec((1, V7X_LANES, tq), lambda b_, h, i: (b_, h, i)),
                  pl.BlockSpec((1, t, V7X_LANES), lambda b_, h, i: (b_, 0, h)),
                  pl.BlockSpec((1, V7X_LANES, t), lambda b_, h, i: (b_, h, 0)),
                  lam_spec, lam_spec, lam_spec, lam_spec],
        out_specs=pl.BlockSpec((1, tq, V7X_LANES), lambda b_, h, i: (b_, i, h)),
        out_shape=jax.ShapeDtypeStruct((b, t, W_DIFF), BF16),
        scratch_shapes=[pltpu.VMEM((n_chains, 1, ATTN_SW), F32), pltpu.VMEM((n_chains, 1, ATTN_SW), F32),
                        pltpu.VMEM((n_chains, DV_DIFF, ATTN_SW), F32),
                        pltpu.VMEM((2, n_chains, ATTN_KB, ATTN_SW), F32),
                        pltpu.VMEM((2, n_chains, ATTN_KB, ATTN_SW), BF16),
                        pltpu.VMEM((2, n_chains, 1, ATTN_SW), F32)],
        compiler_params=pltpu.CompilerParams(
            dimension_semantics=("arbitrary", "arbitrary", "arbitrary"), vmem_limit_bytes=VMEM_LIMIT),
        name="attn_prompt",
    )(qT, kb, vT, *lams)


def _decode_kernel(pt_ref, q_ref, kn_ref, vn_ref, lq1_ref, lk1_ref, lq2_ref, lk2_ref, *rest, pages, ts):
    k_refs, v_refs = rest[:pages], rest[pages:2 * pages]
    o_ref, m_sc, l_sc, acc_sc = rest[2 * pages:]
    s_id = pl.program_id(1)
    q = q_ref[0]
    hr = 2 * ts

    @pl.when(s_id == 0)
    def _():
        m_sc[...] = jnp.full(m_sc.shape, NEG_BIG, F32)
        l_sc[...] = jnp.zeros(l_sc.shape, F32)
        acc_sc[...] = jnp.zeros(acc_sc.shape, F32)

    def update(s, v_heads):
        m_old = m_sc[:, 0:1]
        m_new = jnp.maximum(m_old, jnp.max(s, axis=1, keepdims=True))
        alpha = jnp.exp2(m_old - m_new)
        p = jnp.exp2(s - m_new)
        l_new = alpha * l_sc[:, 0:1] + jnp.sum(p, axis=1, keepdims=True)
        pb = p.astype(BF16)
        for h in range(H_DIFF):
            rows = slice(h * hr, (h + 1) * hr)
            pv = jnp.dot(pb[rows, :], v_heads[h], preferred_element_type=F32)
            acc_sc[rows, :] = acc_sc[rows, :] * alpha[rows, :] + pv
        m_sc[...] = jnp.broadcast_to(m_new, m_sc.shape)
        l_sc[...] = jnp.broadcast_to(l_new, l_sc.shape)

    kT = jnp.concatenate([kr[0].astype(BF16) for kr in k_refs], axis=1)
    s_past = jnp.dot(q, kT, preferred_element_type=F32)
    update(s_past, [jnp.concatenate([vr[0, pl.ds(h, PAGE_SIZE, stride=H_DIFF), :].astype(BF16)
                                     for vr in v_refs], axis=0) for h in range(H_DIFF)])

    @pl.when(s_id == pl.num_programs(1) - 1)
    def _():
        s_new = lax.dot_general(q, kn_ref[0], (((1,), (1,)), ((), ())), preferred_element_type=F32)
        key = lax.broadcasted_iota(jnp.int32, s_new.shape, 1)
        tok = lax.broadcasted_iota(jnp.int32, s_new.shape, 0) % ts
        s_new = jnp.where(key <= tok, s_new, NEG_BIG)
        update(s_new, [vn_ref[0, :, h * DV_DIFF:(h + 1) * DV_DIFF] for h in range(H_DIFF)])
        lam = _diff_lambda(lq1_ref, lk1_ref, lq2_ref, lk2_ref)
        on = acc_sc[...] / l_sc[:, 0:1]
        for h in range(H_DIFF):
            o = on[h * hr:h * hr + ts, :] - lam * on[h * hr + ts:(h + 1) * hr, :]
            of = o * lax.rsqrt(jnp.mean(o * o, axis=-1, keepdims=True) + NORM_EPS) * (1.0 - LAM_INIT)
            o_ref[0, :, h * DV_DIFF:(h + 1) * DV_DIFF] = of


def _attn_sample(q_rows, k_new, v_new, cache_k, cache_v, page_table, lams, *, pages, ts):
    bs, n_pages = page_table.shape
    rows = q_rows.shape[1]
    steps = n_pages // pages
    lam_spec = pl.BlockSpec((1, DH_DIFF), lambda b, s, pt: (0, 0))

    assert 2 * ts == 8, "one head's query rows must fill one sublane tile"

    def page_spec(i):
        return pl.BlockSpec((1, 512, V7X_LANES), lambda b, s, pt: (pt[b, s * pages + i], 0, 0))

    per_b = lambda r: pl.BlockSpec((1, r, 512), lambda b, s, pt: (b, 0, 0))
    grid_spec = pltpu.PrefetchScalarGridSpec(
        num_scalar_prefetch=1,
        grid=(bs, steps),
        in_specs=[per_b(rows), per_b(PAGE_SIZE), per_b(PAGE_SIZE), lam_spec, lam_spec, lam_spec, lam_spec]
                 + [page_spec(i) for i in range(pages)] + [page_spec(i) for i in range(pages)],
        out_specs=per_b(ts),
        scratch_shapes=[pltpu.VMEM((rows, V7X_LANES), F32), pltpu.VMEM((rows, V7X_LANES), F32),
                        pltpu.VMEM((rows, DV_DIFF), F32)],
    )
    return pl.pallas_call(
        functools.partial(_decode_kernel, pages=pages, ts=ts),
        grid_spec=grid_spec,
        out_shape=jax.ShapeDtypeStruct((bs, ts, W_DIFF), F32),
        compiler_params=pltpu.CompilerParams(
            dimension_semantics=("arbitrary", "arbitrary"), vmem_limit_bytes=VMEM_LIMIT),
        name="attn_sample",
    )(page_table, q_rows, k_new, v_new, *lams, *([cache_k] * pages), *([cache_v] * pages))


def _retention_kernel(*refs, chunk, chunk_len, n_chunks, has_r0):
    if has_r0:
        q_ref, k_ref, v_ref, g_ref, r0_ref, o_ref, rout_ref, state = refs
    else:
        q_ref, k_ref, v_ref, g_ref, o_ref, rout_ref, state = refs
    t = pl.program_id(1)

    @pl.when(t == 0)
    def _():
        if has_r0:
            state[...] = r0_ref[0]
        else:
            state[...] = jnp.zeros(state.shape, F32)

    ii = lax.broadcasted_iota(jnp.int32, (chunk, chunk), 0)
    jj = lax.broadcasted_iota(jnp.int32, (chunk, chunk), 1)
    dif = (ii - jj).astype(F32)
    tok = lax.broadcasted_iota(jnp.int32, (chunk, DK_RET), 0).astype(F32)
    for h in range(H_RET):
        cols = slice(h * DK_RET, (h + 1) * DK_RET)
        decay = jnp.where(dif >= 0, jnp.exp(jnp.maximum(dif, 0.0) * LOG_G[h]), 0.0)
        q_dec = jnp.exp((tok + 1.0) * LOG_G[h])
        k_dec = jnp.exp((chunk_len - 1.0 - tok) * LOG_G[h])
        chunk_dec = math.exp(chunk_len * LOG_G[h])
        for c in range(n_chunks):
            rows = slice(c * chunk, (c + 1) * chunk)
            q, k, v = q_ref[0, rows, cols], k_ref[0, rows, cols], v_ref[0, rows, cols]
            r_prev = state[h]
            s = lax.dot_general(q, k, (((1,), (1,)), ((), ())), preferred_element_type=F32) * decay
            o = (jnp.dot(s.astype(q.dtype), v, preferred_element_type=F32)
                 + jnp.dot(q, r_prev.astype(q.dtype), preferred_element_type=F32) * q_dec)
            vd = (v.astype(F32) * k_dec).astype(q.dtype)
            kv = lax.dot_general(k, vd, (((0,), (0,)), ((), ())), preferred_element_type=F32)
            state[h] = r_prev * chunk_dec + kv
            mu = jnp.mean(o, axis=-1, keepdims=True)
            var = jnp.mean(jnp.square(o - mu), axis=-1, keepdims=True)
            nrm = (o - mu) * lax.rsqrt(var + GN_EPS)
            o_ref[0, rows, cols] = (nrm * _silu(g_ref[0, rows, cols].astype(F32))).astype(o_ref.dtype)

    @pl.when(t == pl.num_programs(1) - 1)
    def _():
        rout_ref[0] = state[...]


def _retention(qr, kr, vr, gr, r0, *, chunk, chunk_len, n_chunks):
    b, t, _ = qr.shape
    tr = chunk * n_chunks
    has_r0 = r0 is not None
    tok_spec = pl.BlockSpec((1, tr, 512), lambda b_, i: (b_, i, 0))
    st_spec = pl.BlockSpec((1, H_RET, DK_RET, DV_RET), lambda b_, i: (b_, 0, 0, 0))
    args = [qr, kr, vr, gr] + ([r0] if has_r0 else [])
    return pl.pallas_call(
        functools.partial(_retention_kernel, chunk=chunk, chunk_len=chunk_len, n_chunks=n_chunks,
                          has_r0=has_r0),
        grid=(b, t // tr),
        in_specs=[tok_spec] * 4 + ([st_spec] if has_r0 else []),
        out_specs=[tok_spec, st_spec],
        out_shape=[jax.ShapeDtypeStruct((b, t, W_RET), BF16),
                   jax.ShapeDtypeStruct((b, H_RET, DK_RET, DV_RET), F32)],
        scratch_shapes=[pltpu.VMEM((H_RET, DK_RET, DV_RET), F32)],
        compiler_params=pltpu.CompilerParams(
            dimension_semantics=("arbitrary", "arbitrary"), vmem_limit_bytes=VMEM_LIMIT),
        name="retention",
    )(*args)


def _route(z):
    lane = lax.broadcasted_iota(jnp.int32, z.shape, 1).astype(F32)
    big = 1e9
    neg = -jnp.inf
    is_g = lane < N_GROUPS
    glog = jnp.where(is_g, z, neg)
    gmax = jnp.max(glog, axis=-1, keepdims=True)
    gsel = jnp.min(jnp.where(glog == gmax, lane, big), axis=-1, keepdims=True)
    psum = jnp.sum(jnp.where(is_g, jnp.exp(z - gmax), 0.0), axis=-1, keepdims=True)
    pg_sel = 1.0 / psum
    rel = jnp.zeros_like(z)
    for g in range(N_GROUPS):
        first = ROUTER_E0 + g * EXPERTS_PER_GROUP
        rel = jnp.where(gsel == float(g), pltpu.roll(z, V7X_LANES - first, 1), rel)
    elog = jnp.where(lane < EXPERTS_PER_GROUP, rel, neg)
    v1 = jnp.max(elog, axis=-1, keepdims=True)
    i1 = jnp.min(jnp.where(elog == v1, lane, big), axis=-1, keepdims=True)
    elog2 = jnp.where(lane == i1, neg, elog)
    v2 = jnp.max(elog2, axis=-1, keepdims=True)
    i2 = jnp.min(jnp.where(elog2 == v2, lane, big), axis=-1, keepdims=True)
    e2 = jnp.exp(v2 - v1)
    den = 1.0 + e2
    w1 = (1.0 / den) * pg_sel
    w2 = (e2 / den) * pg_sel
    return gsel, jnp.where(lane == i1, w1, jnp.where(lane == i2, w2, 0.0))


MOE_CHUNK = 160
MOE_PAD = 256


def _ffn_kernel(x_ref, od_ref, rf_ref, beta_ref, wout_ref, gt1_ref, gffn_ref, sc2_ref, sh2_ref, gt2_ref,
                wr_ref, br_ref, tri_ref, wgu_ref, wd_ref, gfin_ref, y_ref,
                h2_sc, g_sc, pos_sc, posr_sc, moe_sc, *, chunk):
    tm = x_ref.shape[1]
    beta = beta_ref[...]
    yd = (od_ref[0].astype(F32) * beta[:, :W_DIFF]).astype(BF16)
    yr = (rf_ref[0].astype(F32) * beta[:, W_DIFF:]).astype(BF16)
    mix = (jnp.dot(yd, wout_ref[:W_DIFF, :], preferred_element_type=F32)
           + jnp.dot(yr, wout_ref[W_DIFF:, :], preferred_element_type=F32))
    x1 = x_ref[0] + gt1_ref[0] * mix
    n2 = x1 * lax.rsqrt(jnp.mean(x1 * x1, axis=-1, keepdims=True) + NORM_EPS)
    h2 = (n2 * gffn_ref[...]) * (1.0 + sc2_ref[0]) + sh2_ref[0]
    h_hi = h2.astype(BF16)
    h_lo = (h2 - h_hi.astype(F32)).astype(BF16)
    zz = jnp.dot(h_hi, wr_ref[...], preferred_element_type=F32)
    z = (zz[:, :V7X_LANES] + zz[:, V7X_LANES:]
         + jnp.dot(h_lo, wr_ref[:, :V7X_LANES], preferred_element_type=F32) + br_ref[...])
    gsel, gates = _route(z)
    h2_sc[...] = h_hi
    g_hi = gates.astype(BF16)
    g_sc[0] = g_hi
    g_sc[1] = (gates - g_hi.astype(F32)).astype(BF16)

    lane = lax.broadcasted_iota(jnp.int32, (tm, V7X_LANES), 1).astype(F32)
    lane1 = lax.broadcasted_iota(jnp.int32, (1, V7X_LANES), 1)
    onehot = jnp.where(lane == gsel, 1.0, 0.0)
    earlier = jnp.dot(tri_ref[...], onehot.astype(BF16), preferred_element_type=F32)
    rank = jnp.sum(onehot * earlier, axis=-1, keepdims=True)
    cnt = jnp.sum(onehot, axis=0, keepdims=True)
    n_chunks = jnp.zeros_like(cnt)
    for m in range(-(-tm // chunk)):
        n_chunks = n_chunks + jnp.where(cnt > m * chunk, 1.0, 0.0)
    first_chunk = [jnp.int32(0)]
    for g in range(N_GROUPS):
        first_chunk.append(first_chunk[-1] + n_chunks[0, g].astype(jnp.int32))
    first_vec = jnp.zeros((1, V7X_LANES), F32)
    for g in range(N_GROUPS):
        first_vec = jnp.where(lane1 == g, first_chunk[g].astype(F32), first_vec)
    pos = jnp.sum(onehot * first_vec, axis=-1, keepdims=True) * chunk + rank
    pos_sc[...] = jnp.broadcast_to(pos, pos_sc.shape)
    posr_sc[...] = jnp.broadcast_to(pos, (tm, V7X_LANES)).T[0:8, :]
    moe_sc[...] = jnp.zeros(moe_sc.shape, F32)

    col_id = lax.broadcasted_iota(jnp.int32, (tm, MOE_PAD), 1)
    col_f = col_id.astype(F32)
    row_f = lax.broadcasted_iota(jnp.int32, (chunk, tm), 0).astype(F32)

    def chunk_body(k, carry):
        g = (jnp.where(k >= first_chunk[1], 1, 0) + jnp.where(k >= first_chunk[2], 1, 0)
             + jnp.where(k >= first_chunk[3], 1, 0))
        off = (k * chunk).astype(F32)
        gather = jnp.where(posr_sc[0:1, :] - off == row_f, 1.0, 0.0).astype(BF16)
        scatter = jnp.where((pos_sc[...] - off == col_f) & (col_id < chunk), 1.0, 0.0).astype(BF16)
        hc = jnp.dot(gather, h2_sc[...], preferred_element_type=F32).astype(BF16)
        gc = (jnp.dot(gather, g_sc[0], preferred_element_type=F32)
              + jnp.dot(gather, g_sc[1], preferred_element_type=F32))
        au = jnp.dot(hc, wgu_ref[g], preferred_element_type=F32)
        gw = EXPERTS_PER_GROUP * D_EXPERT
        hh = []
        for e4 in range(EXPERTS_PER_GROUP):
            a = au[:, e4 * D_EXPERT:(e4 + 1) * D_EXPERT]
            u = au[:, gw + e4 * D_EXPERT:gw + (e4 + 1) * D_EXPERT]
            hh.append((_silu(a) * u * gc[:, e4:e4 + 1]).astype(BF16))
        oc = jnp.dot(jnp.concatenate(hh, axis=1), wd_ref[g], preferred_element_type=F32)
        oc = jnp.concatenate([oc.astype(BF16), jnp.zeros((MOE_PAD - chunk, oc.shape[1]), BF16)], axis=0)
        moe_sc[...] += jnp.dot(scatter, oc, preferred_element_type=F32)
        return carry

    lax.fori_loop(0, first_chunk[N_GROUPS], chunk_body, 0)
    x2 = x1 + gt2_ref[0] * moe_sc[...]
    y_ref[0] = (x2 * lax.rsqrt(jnp.mean(x2 * x2, axis=-1, keepdims=True) + NORM_EPS)) * gfin_ref[...]


def _ffn(x, od, rf, mods, weights, *, tm):
    bx, tx, d = x.shape
    gt1, sc2, sh2, gt2 = mods
    beta, w_out_b, g_ffn, w_router, b_router, wgu_b, wd_b, g_fin = weights
    r = gt1.shape[1]
    chunk = min(MOE_CHUNK, tm)
    tri = jnp.asarray(np.tril(np.ones((tm, tm), np.float32), -1), BF16)
    mod_spec = pl.BlockSpec((1, r, d), (lambda b, t: (b, 0, 0)) if r == 1 else (lambda b, t: (b, t, 0)))
    const = lambda shape: pl.BlockSpec(shape, lambda b, t: (0,) * len(shape), pipeline_mode=pl.Buffered(1))
    tok = lambda w: pl.BlockSpec((1, tm, w), lambda b, t: (b, t, 0))
    gw = EXPERTS_PER_GROUP * D_EXPERT
    return pl.pallas_call(
        functools.partial(_ffn_kernel, chunk=chunk),
        grid=(bx, tx // tm),
        in_specs=[tok(d), tok(W_DIFF), tok(W_RET), const((1, d)), const((d, d)), mod_spec, const((1, d)),
                  mod_spec, mod_spec, mod_spec, const((d, 2 * V7X_LANES)), const((1, V7X_LANES)),
                  const((tm, tm)), const((N_GROUPS, d, 2 * gw)), const((N_GROUPS, gw, d)), const((1, d))],
        out_specs=tok(d),
        out_shape=jax.ShapeDtypeStruct((bx, tx, d), F32),
        scratch_shapes=[pltpu.VMEM((tm, d), BF16), pltpu.VMEM((2, tm, V7X_LANES), BF16),
                        pltpu.VMEM((tm, MOE_PAD), F32), pltpu.VMEM((8, tm), F32),
                        pltpu.VMEM((tm, d), F32)],
        compiler_params=pltpu.CompilerParams(
            dimension_semantics=("arbitrary", "arbitrary"), vmem_limit_bytes=VMEM_LIMIT),
        name="ffn",
    )(x, od, rf, beta, w_out_b, gt1, g_ffn, sc2, sh2, gt2, w_router, b_router, tri, wgu_b, wd_b, g_fin)


def kernel(x_prompt, x_sample, cache_k, cache_v, state_ret, page_table, c_prompt, c_sample, w_ada, b_ada,
           norm_mix_g, norm_ffn_g, w_in, lambda_q1, lambda_k1, lambda_q2, lambda_k2, beta_mix, w_out,
           w_group, b_group, w_expert_router, b_expert_router, w_gate_e, w_up_e, w_down_e, final_g):
    assert w_ada.shape[0] == 1, "single-layer stack only"
    bp, tp, d = x_prompt.shape
    bs, ts, _ = x_sample.shape
    n_pages = page_table.shape[1]
    past_len = n_pages * PAGE_SIZE
    n_phys = cache_k.shape[1]

    w_in_b = w_in[0].astype(BF16)
    w_out_b = w_out[0].astype(BF16)
    gw = EXPERTS_PER_GROUP * D_EXPERT

    def by_group(w):
        return jnp.transpose(w.reshape(N_GROUPS, EXPERTS_PER_GROUP, d, D_EXPERT), (0, 2, 1, 3)).reshape(
            N_GROUPS, d, gw)

    wgu_b = jnp.concatenate([by_group(w_gate_e[0]), by_group(w_up_e[0])], axis=-1).astype(BF16)
    wd_b = w_down_e[0].reshape(N_GROUPS, gw, d).astype(BF16)
    w_er = jnp.transpose(w_expert_router[0], (1, 0, 2)).reshape(d, N_EXPERTS)
    pad = V7X_LANES - N_GROUPS - N_EXPERTS
    w_router = jnp.concatenate([w_group[0], w_er, jnp.zeros((d, pad), F32)], axis=1)
    w_router_hi = w_router.astype(BF16)
    w_router = jnp.concatenate([w_router_hi, (w_router - w_router_hi.astype(F32)).astype(BF16)], axis=1)
    b_router = jnp.concatenate([b_group[0], b_expert_router[0].reshape(N_EXPERTS), jnp.zeros((pad,), F32)])[None]
    ffn_weights = (beta_mix, w_out_b, norm_ffn_g, w_router, b_router, wgu_b, wd_b, final_g[None])
    lams = (lambda_q1, lambda_k1, lambda_q2, lambda_k2)

    mod = _adaln(jnp.concatenate([c_prompt, c_sample], axis=0), w_ada[0], b_ada[0])
    mp = [m[:, None, :] for m in jnp.split(mod[:bp], 6, axis=-1)]
    ms = [jnp.repeat(m, ts, axis=0)[None] for m in jnp.split(mod[bp:], 6, axis=-1)]

    tabs_p = _rope_tables(jnp.arange(tp, dtype=F32))
    kp, vp, qT, kb, vT, qr, kr, vr, gr = _inproj(
        x_prompt, norm_mix_g, mp[1], mp[0], w_in_b, tabs_p, tm=512, transposed=True)
    od_p = _attn_prompt(qT, kb, vT, lams, tq=ATTN_TQ)
    rf_p, ret_p = _retention(qr, kr, vr, gr, None, chunk=RET_CHUNK, chunk_len=RET_CHUNK, n_chunks=4)
    y_p = _ffn(x_prompt, od_p, rf_p, (mp[2], mp[4], mp[3], mp[5]), ffn_weights, tm=512)

    rows_s = bs * ts
    tabs_s = _rope_tables(jnp.tile(past_len + jnp.arange(ts, dtype=F32), bs))
    ks, vs, q_s, kb_s, vb_s, qr_s, kr_s, vr_s, gr_s = _inproj(
        x_sample.reshape(1, rows_s, d), norm_mix_g, ms[1], ms[0], w_in_b, tabs_s, tm=rows_s, transposed=False)
    sub = jnp.arange(2 * H_DIFF)
    col_sub = jnp.arange(512) // DH_DIFF
    q_rows = jnp.where((sub[:, None, None] == col_sub[None, None, :])[None],
                       q_s.reshape(bs, 1, ts, 512), jnp.zeros((), BF16)).reshape(bs, 2 * H_DIFF * ts, 512)
    pad_keys = lambda a: jnp.pad(a.reshape(bs, ts, 512), ((0, 0), (0, PAGE_SIZE - ts), (0, 0)))
    ck = jnp.transpose(cache_k[0], (0, 2, 3, 1)).reshape(n_phys, 2 * H_DIFF * DH_DIFF, PAGE_SIZE)
    cv = cache_v[0].reshape(n_phys, PAGE_SIZE * H_DIFF, DV_DIFF)
    od_s = _attn_sample(q_rows, pad_keys(kb_s), pad_keys(vb_s), ck, cv, page_table, lams, pages=16, ts=ts)
    chunk_s = 16
    pad_tok = lambda a: jnp.pad(a.reshape(bs, ts, 512), ((0, 0), (0, chunk_s - ts), (0, 0)))
    rf_s, ret_s = _retention(pad_tok(qr_s), pad_tok(kr_s), pad_tok(vr_s), pad_tok(gr_s), state_ret[0],
                             chunk=chunk_s, chunk_len=ts, n_chunks=1)
    y_s = _ffn(x_sample.reshape(1, rows_s, d), od_s.reshape(1, rows_s, 512),
               rf_s[:, :ts].reshape(1, rows_s, 512), (ms[2], ms[4], ms[3], ms[5]), ffn_weights, tm=rows_s)

    return (y_p, y_s.reshape(bs, ts, d),
            kp.reshape(1, bp, tp, 2 * H_DIFF, DH_DIFF), vp.reshape(1, bp, tp, H_DIFF, DV_DIFF), ret_p[None],
            ks.reshape(1, bs, ts, 2 * H_DIFF, DH_DIFF), vs.reshape(1, bs, ts, H_DIFF, DV_DIFF), ret_s[None])
```

```python
import functools
import math

import jax
import jax.numpy as jnp
import numpy as np
from jax import lax
from jax.experimental import pallas as pl
from jax.experimental.pallas import tpu as pltpu

F32 = jnp.float32
BF16 = jnp.bfloat16

V7X_LANES = 128
V7X_VMEM_BYTES = 64 * 1024 * 1024
VMEM_LIMIT = 56 * 1024 * 1024

D_MODEL = 1024
PAGE_SIZE = 128
H_DIFF = 4
DH_DIFF = 64
DV_DIFF = 128
W_DIFF = H_DIFF * DV_DIFF
H_RET = 4
DK_RET = 128
DV_RET = 128
W_RET = H_RET * DV_RET
IN_COLS = 7 * 512
RET_CHUNK = 128
ROPE_THETA = 10000.0
N_GROUPS = 4
EXPERTS_PER_GROUP = 4
N_EXPERTS = 16
D_EXPERT = 256
NORM_EPS = 1e-6
GN_EPS = 1e-5
LAM_INIT = 0.8 - 0.6 * math.exp(-0.3 * 0)
LOG2E = 1.4426950408889634
NEG_BIG = -1e30
LOG_G = tuple(math.log1p(-2.0 ** (-5.0 - h)) for h in range(H_RET))
ROUTER_E0 = N_GROUPS


def _silu(x):
    return x * jax.nn.sigmoid(x)


def _adaln_kernel(c_ref, w_ref, b_ref, o_ref):
    s = _silu(c_ref[...])
    o_ref[...] = jnp.dot(s, w_ref[...], preferred_element_type=F32,
                         precision=lax.Precision.HIGHEST) + b_ref[...]


def _adaln(c, w_ada, b_ada):
    n, d = c.shape
    cols = w_ada.shape[1]
    bn = 1024
    return pl.pallas_call(
        _adaln_kernel,
        grid=(cols // bn,),
        in_specs=[pl.BlockSpec((n, d), lambda j: (0, 0)),
                  pl.BlockSpec((d, bn), lambda j: (0, j)),
                  pl.BlockSpec((1, bn), lambda j: (0, j))],
        out_specs=pl.BlockSpec((n, bn), lambda j: (0, j)),
        out_shape=jax.ShapeDtypeStruct((n, cols), F32),
        name="adaln",
    )(c, w_ada, b_ada.reshape(1, cols))


def _rope64(z, cos, sin_signed, first_half):
    rot = jnp.where(first_half, pltpu.roll(z, 96, 1), pltpu.roll(z, 32, 1))
    return z * cos + rot * sin_signed


def _rope128(z, cos, sin_signed):
    return z * cos + pltpu.roll(z, 64, 1) * sin_signed


def _inproj_kernel(x_ref, g_ref, sc_ref, sh_ref, w_ref, cd_ref, sd_ref, cr_ref, sr_ref,
                   kout_ref, vout_ref, q_ref, kb_ref, v_ref, *rest, transposed, fuse_retention):
    if fuse_retention:
        rf_ref, rout_ref, state = rest
        t = pl.program_id(1)

        @pl.when(t == 0)
        def _():
            state[...] = jnp.zeros(state.shape, F32)
    else:
        qr_ref, kr_ref, vr_ref, gr_ref = rest
    x = x_ref[0]
    y = x * lax.rsqrt(jnp.mean(x * x, axis=-1, keepdims=True) + NORM_EPS)
    h = (y * g_ref[...]) * (1.0 + sc_ref[0]) + sh_ref[0]
    hb = h.astype(BF16)
    tm = x.shape[0]

    def proj(section):
        return jnp.dot(hb, w_ref[:, section * 512:(section + 1) * 512], preferred_element_type=F32)

    zq, zk, zv, zqr, zkr, zvr, zgr = [proj(i) for i in range(7)]

    cd, sd, cr, sr = cd_ref[...], sd_ref[...], cr_ref[...], sr_ref[...]
    lane = lax.broadcasted_iota(jnp.int32, (tm, V7X_LANES), 1)
    first_half = (lane % DH_DIFF) < (DH_DIFF // 2)
    q_scale = (DH_DIFF ** -0.5) * LOG2E
    for j in range(4):
        cols = slice(j * V7X_LANES, (j + 1) * V7X_LANES)
        q = _rope64(zq[:, cols], cd, sd, first_half) * q_scale
        k = _rope64(zk[:, cols], cd, sd, first_half)
        v = zv[:, cols]
        kout_ref[0, :, cols] = k
        vout_ref[0, pl.ds(j, tm, stride=H_DIFF), :] = v
        kb_ref[0, :, cols] = k.astype(BF16)
        if transposed:
            q_ref[0, cols, :] = q.T.astype(BF16)
            v_ref[0, cols, :] = v.T.astype(BF16)
        else:
            q_ref[0, :, cols] = q.astype(BF16)
            v_ref[0, :, cols] = v.astype(BF16)
        qr = _rope128(zqr[:, cols], cr, sr).astype(BF16)
        kr = (_rope128(zkr[:, cols], cr, sr) * (DK_RET ** -0.5)).astype(BF16)
        vr = zvr[:, cols].astype(BF16)
        gr = zgr[:, cols].astype(BF16)
        if fuse_retention:
            def put(rows, val, cols=cols):
                rf_ref[0, rows, cols] = val.astype(rf_ref.dtype)

            _retention_head(qr, kr, vr, gr, state, j, put, chunk=RET_CHUNK, chunk_len=RET_CHUNK,
                            n_chunks=tm // RET_CHUNK)
        else:
            qr_ref[0, :, cols] = qr
            kr_ref[0, :, cols] = kr
            vr_ref[0, :, cols] = vr
            gr_ref[0, :, cols] = gr

    if fuse_retention:
        @pl.when(t == pl.num_programs(1) - 1)
        def _():
            rout_ref[0] = state[...]


def _inproj(x, g, sc, sh, w_in_b, tables, *, tm, transposed, fuse_retention):
    bx, tx, d = x.shape
    r = sc.shape[1]
    nt = tx // tm
    mod_spec = pl.BlockSpec((1, r, d), (lambda b, t: (b, 0, 0)) if r == 1 else (lambda b, t: (b, t, 0)))
    tab_spec = pl.BlockSpec((tm, V7X_LANES), lambda b, t: (t, 0))
    nat = lambda dt: jax.ShapeDtypeStruct((bx, tx, 512), dt)
    nat_spec = pl.BlockSpec((1, tm, 512), lambda b, t: (b, t, 0))
    if transposed:
        tr = jax.ShapeDtypeStruct((bx, 512, tx), BF16)
        tr_spec = pl.BlockSpec((1, 512, tm), lambda b, t: (b, 0, t))
    else:
        tr, tr_spec = nat(BF16), nat_spec
    vrows = jax.ShapeDtypeStruct((bx, tx * H_DIFF, DV_DIFF), F32)
    vrows_spec = pl.BlockSpec((1, tm * H_DIFF, DV_DIFF), lambda b, t: (b, t, 0))
    out_shape = [nat(F32), vrows, tr, nat(BF16), tr]
    out_specs = [nat_spec, vrows_spec, tr_spec, nat_spec, tr_spec]
    scratch = []
    if fuse_retention:
        out_shape += [nat(BF16), jax.ShapeDtypeStruct((bx, H_RET, DK_RET, DV_RET), F32)]
        out_specs += [nat_spec, pl.BlockSpec((1, H_RET, DK_RET, DV_RET), lambda b, t: (b, 0, 0, 0))]
        scratch = [pltpu.VMEM((H_RET, DK_RET, DV_RET), F32)]
    else:
        out_shape += [nat(BF16)] * 4
        out_specs += [nat_spec] * 4
    return pl.pallas_call(
        functools.partial(_inproj_kernel, transposed=transposed, fuse_retention=fuse_retention),
        grid=(bx, nt),
        scratch_shapes=scratch,
        in_specs=[pl.BlockSpec((1, tm, d), lambda b, t: (b, t, 0)),
                  pl.BlockSpec((1, d), lambda b, t: (0, 0)),
                  mod_spec, mod_spec,
                  pl.BlockSpec((d, IN_COLS), lambda b, t: (0, 0)),
                  tab_spec, tab_spec, tab_spec, tab_spec],
        out_specs=out_specs,
        out_shape=out_shape,
        compiler_params=pltpu.CompilerParams(
            dimension_semantics=("arbitrary", "arbitrary"), vmem_limit_bytes=VMEM_LIMIT),
        name="inproj",
    )(x, g, sc, sh, w_in_b, *tables)


def _rope_tables(pos):
    lane = np.arange(V7X_LANES)

    def one(d):
        half = d // 2
        inv = jnp.power(ROPE_THETA, -(2.0 / d) * jnp.arange(half, dtype=F32))
        ang = pos[:, None] * inv[None, :]
        idx = lane % half
        sign = jnp.asarray(np.where((lane % d) < half, -1.0, 1.0), F32)
        return jnp.cos(ang)[:, idx], jnp.sin(ang)[:, idx] * sign[None, :]

    cd, sd = one(DH_DIFF)
    cr, sr = one(DK_RET)
    return cd, sd, cr, sr


def _diff_lambda(lq1_ref, lk1_ref, lq2_ref, lk2_ref):
    e1 = jnp.exp(jnp.sum(lq1_ref[...] * lk1_ref[...], axis=-1, keepdims=True))
    e2 = jnp.exp(jnp.sum(lq2_ref[...] * lk2_ref[...], axis=-1, keepdims=True))
    return e1 - e2 + LAM_INIT


ATTN_SW = 256
ATTN_KB = 256
ATTN_TQ = 2 * ATTN_KB
ATTN_ONES_ROWS = 16


def _attn_kernel(qT_ref, k_ref, vT_ref, lq1_ref, lk1_ref, lq2_ref, lk2_ref, o_ref,
                 m_sc, l_sc, acc_sc, s_buf, p_buf, a_buf, *, tq):
    sw, kb = ATTN_SW, ATTN_KB
    assert tq == 2 * kb and tq % sw == 0
    ns = tq // sw
    nh = qT_ref.shape[1] // V7X_LANES
    qi = pl.program_id(2)
    row = lax.broadcasted_iota(jnp.int32, (V7X_LANES, tq), 0)
    zero = jnp.zeros((V7X_LANES, tq), BF16)
    q_sub = []
    for hh in range(nh):
        qT = qT_ref[0, hh * V7X_LANES:(hh + 1) * V7X_LANES, :]
        q_sub.append((jnp.where(row < DH_DIFF, qT, zero), jnp.where(row >= DH_DIFF, qT, zero)))
    m_sc[...] = jnp.full(m_sc.shape, NEG_BIG, F32)
    l_sc[...] = jnp.zeros(l_sc.shape, F32)
    acc_sc[...] = jnp.zeros(acc_sc.shape, F32)
    chains = [(hh, a, st) for hh in range(nh) for a in range(2) for st in range(ns)]
    chain_id = lambda hh, a, st: (hh * 2 + a) * ns + st
    head_lanes = lambda hh: slice(hh * V7X_LANES, (hh + 1) * V7X_LANES)

    def scores(slot, key_base, diag_step):
        key0 = pl.multiple_of(key_base, kb)
        for hh, a, st in chains:
            kblk = k_ref[0, pl.ds(key0, kb), head_lanes(hh)]
            s = jnp.dot(kblk, q_sub[hh][a][:, st * sw:(st + 1) * sw], preferred_element_type=F32)
            if diag_step is not None:
                off = diag_step * kb - st * sw
                if off + kb - 1 > 0:
                    kpos = off + lax.broadcasted_iota(jnp.int32, (kb, sw), 0)
                    qpos = lax.broadcasted_iota(jnp.int32, (kb, sw), 1)
                    s = jnp.where(kpos <= qpos, s, -jnp.inf)
            s_buf[slot, chain_id(hh, a, st)] = s

    def softmax(slot):
        for hh, a, st in chains:
            c = chain_id(hh, a, st)
            m_old = m_sc[c]
            m_new = jnp.maximum(m_old, jnp.max(s_buf[slot, c], axis=0, keepdims=True))
            alpha = jnp.exp2(m_old - m_new)
            p = jnp.exp2(s_buf[slot, c] - m_new)
            m_sc[c] = m_new
            a_buf[slot, c] = alpha
            p_buf[slot, c] = p.astype(BF16)

    def values(slot, key_base):
        key0 = pl.multiple_of(key_base, kb)
        ones = jnp.ones((ATTN_ONES_ROWS, kb), BF16)
        for hh, a, st in chains:
            c = chain_id(hh, a, st)
            vblk = jnp.concatenate([vT_ref[0, head_lanes(hh), pl.ds(key0, kb)], ones], axis=0)
            pv = jnp.dot(vblk, p_buf[slot, c], preferred_element_type=F32)
            alpha = a_buf[slot, c]
            acc_sc[c] = acc_sc[c] * alpha + pv[:DV_DIFF]
            l_sc[c] = l_sc[c] * alpha + pv[DV_DIFF:DV_DIFF + 1]

    diag_base = [qi * tq + d * kb for d in range(2)]
    scores(0, diag_base[0], 0)
    scores(1, diag_base[1], 1)
    softmax(0)

    def trip_pair(u, carry):
        for par_t in range(2):
            full_base = (2 * u + par_t) * kb
            lag_base = jnp.where(u == 0, diag_base[par_t], full_base - 2 * kb)
            scores(par_t, full_base, None)
            softmax(1 - par_t)
            values(par_t, lag_base)
        return carry

    lax.fori_loop(0, qi, trip_pair, 0)
    last = [jnp.where(qi == 0, diag_base[d], (2 * qi - 2 + d) * kb) for d in range(2)]
    softmax(1)
    values(0, last[0])
    values(1, last[1])

    lam = _diff_lambda(lq1_ref, lk1_ref, lq2_ref, lk2_ref)
    for hh in range(nh):
        for st in range(ns):
            outs = [acc_sc[chain_id(hh, a, st)] / l_sc[chain_id(hh, a, st)] for a in range(2)]
            o = outs[0] - lam * outs[1]
            of = o * lax.rsqrt(jnp.mean(o * o, axis=0, keepdims=True) + NORM_EPS) * (1.0 - LAM_INIT)
            o_ref[0, st * sw:(st + 1) * sw, head_lanes(hh)] = of.T.astype(o_ref.dtype)


def _attn_prompt(qT, kb, vT, lams, *, tq, heads):
    b, _, t = qT.shape
    lam_spec = pl.BlockSpec((1, DH_DIFF), lambda b_, h, i: (0, 0))
    n_chains = heads * 2 * (tq // ATTN_SW)
    hw = heads * V7X_LANES
    return pl.pallas_call(
        functools.partial(_attn_kernel, tq=tq),
        grid=(b, H_DIFF // heads, t // tq),
        in_specs=[pl.BlockSpec((1, hw, tq), lambda b_, h, i: (b_, h, i)),
                  pl.BlockSpec((1, t, hw), lambda b_, h, i: (b_, 0, h)),
                  pl.BlockSpec((1, hw, t), lambda b_, h, i: (b_, h, 0)),
                  lam_spec, lam_spec, lam_spec, lam_spec],
        out_specs=pl.BlockSpec((1, tq, hw), lambda b_, h, i: (b_, i, h)),
        out_shape=jax.ShapeDtypeStruct((b, t, W_DIFF), BF16),
        scratch_shapes=[pltpu.VMEM((n_chains, 1, ATTN_SW), F32), pltpu.VMEM((n_chains, 1, ATTN_SW), F32),
                        pltpu.VMEM((n_chains, DV_DIFF, ATTN_SW), F32),
                        pltpu.VMEM((2, n_chains, ATTN_KB, ATTN_SW), F32),
                        pltpu.VMEM((2, n_chains, ATTN_KB, ATTN_SW), BF16),
                        pltpu.VMEM((2, n_chains, 1, ATTN_SW), F32)],
        compiler_params=pltpu.CompilerParams(
            dimension_semantics=("arbitrary", "arbitrary", "arbitrary"), vmem_limit_bytes=VMEM_LIMIT),
        name="attn_prompt",
    )(qT, kb, vT, *lams)


def _decode_kernel(pt_ref, q_ref, kn_ref, vn_ref, lq1_ref, lk1_ref, lq2_ref, lk2_ref, *rest, pages, ts):
    k_refs, v_refs = rest[:pages], rest[pages:2 * pages]
    o_ref, m_sc, l_sc, acc_sc = rest[2 * pages:]
    s_id = pl.program_id(1)
    q = q_ref[0]
    hr = 2 * ts

    @pl.when(s_id == 0)
    def _():
        m_sc[...] = jnp.full(m_sc.shape, NEG_BIG, F32)
        l_sc[...] = jnp.zeros(l_sc.shape, F32)
        acc_sc[...] = jnp.zeros(acc_sc.shape, F32)

    def update(s, v_heads):
        m_old = m_sc[:, 0:1]
        m_new = jnp.maximum(m_old, jnp.max(s, axis=1, keepdims=True))
        alpha = jnp.exp2(m_old - m_new)
        p = jnp.exp2(s - m_new)
        l_new = alpha * l_sc[:, 0:1] + jnp.sum(p, axis=1, keepdims=True)
        pb = p.astype(BF16)
        for h in range(H_DIFF):
            rows = slice(h * hr, (h + 1) * hr)
            pv = jnp.dot(pb[rows, :], v_heads[h], preferred_element_type=F32)
            acc_sc[rows, :] = acc_sc[rows, :] * alpha[rows, :] + pv
        m_sc[...] = jnp.broadcast_to(m_new, m_sc.shape)
        l_sc[...] = jnp.broadcast_to(l_new, l_sc.shape)

    kT = jnp.concatenate([kr[0].astype(BF16) for kr in k_refs], axis=1)
    s_past = jnp.dot(q, kT, preferred_element_type=F32)
    update(s_past, [jnp.concatenate([vr[0, pl.ds(h, PAGE_SIZE, stride=H_DIFF), :].astype(BF16)
                                     for vr in v_refs], axis=0) for h in range(H_DIFF)])

    @pl.when(s_id == pl.num_programs(1) - 1)
    def _():
        s_new = lax.dot_general(q, kn_ref[0], (((1,), (1,)), ((), ())), preferred_element_type=F32)
        key = lax.broadcasted_iota(jnp.int32, s_new.shape, 1)
        tok = lax.broadcasted_iota(jnp.int32, s_new.shape, 0) % ts
        s_new = jnp.where(key <= tok, s_new, NEG_BIG)
        update(s_new, [vn_ref[0, :, h * DV_DIFF:(h + 1) * DV_DIFF] for h in range(H_DIFF)])
        lam = _diff_lambda(lq1_ref, lk1_ref, lq2_ref, lk2_ref)
        on = acc_sc[...] / l_sc[:, 0:1]
        for h in range(H_DIFF):
            o = on[h * hr:h * hr + ts, :] - lam * on[h * hr + ts:(h + 1) * hr, :]
            of = o * lax.rsqrt(jnp.mean(o * o, axis=-1, keepdims=True) + NORM_EPS) * (1.0 - LAM_INIT)
            o_ref[0, :, h * DV_DIFF:(h + 1) * DV_DIFF] = of


def _attn_sample(q_rows, k_new, v_new, cache_k, cache_v, page_table, lams, *, pages, ts):
    bs, n_pages = page_table.shape
    rows = q_rows.shape[1]
    steps = n_pages // pages
    lam_spec = pl.BlockSpec((1, DH_DIFF), lambda b, s, pt: (0, 0))

    assert 2 * ts == 8, "one head's query rows must fill one sublane tile"

    def page_spec(i):
        return pl.BlockSpec((1, 512, V7X_LANES), lambda b, s, pt: (pt[b, s * pages + i], 0, 0))

    per_b = lambda r: pl.BlockSpec((1, r, 512), lambda b, s, pt: (b, 0, 0))
    grid_spec = pltpu.PrefetchScalarGridSpec(
        num_scalar_prefetch=1,
        grid=(bs, steps),
        in_specs=[per_b(rows), per_b(PAGE_SIZE), per_b(PAGE_SIZE), lam_spec, lam_spec, lam_spec, lam_spec]
                 + [page_spec(i) for i in range(pages)] + [page_spec(i) for i in range(pages)],
        out_specs=per_b(ts),
        scratch_shapes=[pltpu.VMEM((rows, V7X_LANES), F32), pltpu.VMEM((rows, V7X_LANES), F32),
                        pltpu.VMEM((rows, DV_DIFF), F32)],
    )
    return pl.pallas_call(
        functools.partial(_decode_kernel, pages=pages, ts=ts),
        grid_spec=grid_spec,
        out_shape=jax.ShapeDtypeStruct((bs, ts, W_DIFF), F32),
        compiler_params=pltpu.CompilerParams(
            dimension_semantics=("arbitrary", "arbitrary"), vmem_limit_bytes=VMEM_LIMIT),
        name="attn_sample",
    )(page_table, q_rows, k_new, v_new, *lams, *([cache_k] * pages), *([cache_v] * pages))


def _retention_head(q, k, v, g, state, h, put, *, chunk, chunk_len, n_chunks):
    ii = lax.broadcasted_iota(jnp.int32, (chunk, chunk), 0)
    jj = lax.broadcasted_iota(jnp.int32, (chunk, chunk), 1)
    dif = (ii - jj).astype(F32)
    tok = lax.broadcasted_iota(jnp.int32, (chunk, DK_RET), 0).astype(F32)
    decay = jnp.where(dif >= 0, jnp.exp(jnp.maximum(dif, 0.0) * LOG_G[h]), 0.0)
    q_dec = jnp.exp((tok + 1.0) * LOG_G[h])
    k_dec = jnp.exp((chunk_len - 1.0 - tok) * LOG_G[h])
    chunk_dec = math.exp(chunk_len * LOG_G[h])
    for c in range(n_chunks):
        rows = slice(c * chunk, (c + 1) * chunk)
        qc, kc, vc = q[rows], k[rows], v[rows]
        r_prev = state[h]
        s = lax.dot_general(qc, kc, (((1,), (1,)), ((), ())), preferred_element_type=F32) * decay
        o = (jnp.dot(s.astype(qc.dtype), vc, preferred_element_type=F32)
             + jnp.dot(qc, r_prev.astype(qc.dtype), preferred_element_type=F32) * q_dec)
        vd = (vc.astype(F32) * k_dec).astype(qc.dtype)
        kv = lax.dot_general(kc, vd, (((0,), (0,)), ((), ())), preferred_element_type=F32)
        state[h] = r_prev * chunk_dec + kv
        mu = jnp.mean(o, axis=-1, keepdims=True)
        var = jnp.mean(jnp.square(o - mu), axis=-1, keepdims=True)
        nrm = (o - mu) * lax.rsqrt(var + GN_EPS)
        put(rows, nrm * _silu(g[rows].astype(F32)))


def _retention_kernel(q_ref, k_ref, v_ref, g_ref, r0_ref, o_ref, rout_ref, state, *, chunk, chunk_len, n_chunks):
    t = pl.program_id(1)

    @pl.when(t == 0)
    def _():
        state[...] = r0_ref[0]

    for h in range(H_RET):
        cols = slice(h * DK_RET, (h + 1) * DK_RET)

        def put(rows, val, cols=cols):
            o_ref[0, rows, cols] = val.astype(o_ref.dtype)

        _retention_head(q_ref[0, :, cols], k_ref[0, :, cols], v_ref[0, :, cols], g_ref[0, :, cols],
                        state, h, put, chunk=chunk, chunk_len=chunk_len, n_chunks=n_chunks)

    @pl.when(t == pl.num_programs(1) - 1)
    def _():
        rout_ref[0] = state[...]


def _retention(qr, kr, vr, gr, r0, *, chunk, chunk_len, n_chunks):
    b, t, _ = qr.shape
    tr = chunk * n_chunks
    tok_spec = pl.BlockSpec((1, tr, 512), lambda b_, i: (b_, i, 0))
    st_spec = pl.BlockSpec((1, H_RET, DK_RET, DV_RET), lambda b_, i: (b_, 0, 0, 0))
    return pl.pallas_call(
        functools.partial(_retention_kernel, chunk=chunk, chunk_len=chunk_len, n_chunks=n_chunks),
        grid=(b, t // tr),
        in_specs=[tok_spec] * 4 + [st_spec],
        out_specs=[tok_spec, st_spec],
        out_shape=[jax.ShapeDtypeStruct((b, t, W_RET), BF16),
                   jax.ShapeDtypeStruct((b, H_RET, DK_RET, DV_RET), F32)],
        scratch_shapes=[pltpu.VMEM((H_RET, DK_RET, DV_RET), F32)],
        compiler_params=pltpu.CompilerParams(
            dimension_semantics=("arbitrary", "arbitrary"), vmem_limit_bytes=VMEM_LIMIT),
        name="retention",
    )(qr, kr, vr, gr, r0)


def _route(z):
    lane = lax.broadcasted_iota(jnp.int32, z.shape, 1).astype(F32)
    big = 1e9
    neg = -jnp.inf
    is_g = lane < N_GROUPS
    glog = jnp.where(is_g, z, neg)
    gmax = jnp.max(glog, axis=-1, keepdims=True)
    gsel = jnp.min(jnp.where(glog == gmax, lane, big), axis=-1, keepdims=True)
    psum = jnp.sum(jnp.where(is_g, jnp.exp(z - gmax), 0.0), axis=-1, keepdims=True)
    pg_sel = 1.0 / psum
    rel = jnp.zeros_like(z)
    for g in range(N_GROUPS):
        first = ROUTER_E0 + g * EXPERTS_PER_GROUP
        rel = jnp.where(gsel == float(g), pltpu.roll(z, V7X_LANES - first, 1), rel)
    elog = jnp.where(lane < EXPERTS_PER_GROUP, rel, neg)
    v1 = jnp.max(elog, axis=-1, keepdims=True)
    i1 = jnp.min(jnp.where(elog == v1, lane, big), axis=-1, keepdims=True)
    elog2 = jnp.where(lane == i1, neg, elog)
    v2 = jnp.max(elog2, axis=-1, keepdims=True)
    i2 = jnp.min(jnp.where(elog2 == v2, lane, big), axis=-1, keepdims=True)
    e2 = jnp.exp(v2 - v1)
    den = 1.0 + e2
    w1 = (1.0 / den) * pg_sel
    w2 = (e2 / den) * pg_sel
    return gsel, jnp.where(lane == i1, w1, jnp.where(lane == i2, w2, 0.0))


MOE_CHUNK = 160
MOE_PAD = 256


def _ffn_kernel(x_ref, od_ref, rf_ref, beta_ref, wout_ref, gt1_ref, gffn_ref, sc2_ref, sh2_ref, gt2_ref,
                wr_ref, br_ref, tri_ref, wgu_ref, wd_ref, gfin_ref, y_ref,
                h2_sc, g_sc, pos_sc, posr_sc, moe_sc, *, chunk):
    tm = x_ref.shape[1]
    beta = beta_ref[...]
    yd = (od_ref[0].astype(F32) * beta[:, :W_DIFF]).astype(BF16)
    yr = (rf_ref[0].astype(F32) * beta[:, W_DIFF:]).astype(BF16)
    mix = (jnp.dot(yd, wout_ref[:W_DIFF, :], preferred_element_type=F32)
           + jnp.dot(yr, wout_ref[W_DIFF:, :], preferred_element_type=F32))
    x1 = x_ref[0] + gt1_ref[0] * mix
    n2 = x1 * lax.rsqrt(jnp.mean(x1 * x1, axis=-1, keepdims=True) + NORM_EPS)
    h2 = (n2 * gffn_ref[...]) * (1.0 + sc2_ref[0]) + sh2_ref[0]
    h_hi = h2.astype(BF16)
    h_lo = (h2 - h_hi.astype(F32)).astype(BF16)
    zz = jnp.dot(h_hi, wr_ref[...], preferred_element_type=F32)
    z = (zz[:, :V7X_LANES] + zz[:, V7X_LANES:]
         + jnp.dot(h_lo, wr_ref[:, :V7X_LANES], preferred_element_type=F32) + br_ref[...])
    gsel, gates = _route(z)
    h2_sc[...] = h_hi
    g_hi = gates.astype(BF16)
    g_sc[0] = g_hi
    g_sc[1] = (gates - g_hi.astype(F32)).astype(BF16)

    lane = lax.broadcasted_iota(jnp.int32, (tm, V7X_LANES), 1).astype(F32)
    lane1 = lax.broadcasted_iota(jnp.int32, (1, V7X_LANES), 1)
    onehot = jnp.where(lane == gsel, 1.0, 0.0)
    earlier = jnp.dot(tri_ref[...], onehot.astype(BF16), preferred_element_type=F32)
    rank = jnp.sum(onehot * earlier, axis=-1, keepdims=True)
    cnt = jnp.sum(onehot, axis=0, keepdims=True)
    n_chunks = jnp.zeros_like(cnt)
    for m in range(-(-tm // chunk)):
        n_chunks = n_chunks + jnp.where(cnt > m * chunk, 1.0, 0.0)
    first_chunk = [jnp.int32(0)]
    for g in range(N_GROUPS):
        first_chunk.append(first_chunk[-1] + n_chunks[0, g].astype(jnp.int32))
    first_vec = jnp.zeros((1, V7X_LANES), F32)
    for g in range(N_GROUPS):
        first_vec = jnp.where(lane1 == g, first_chunk[g].astype(F32), first_vec)
    pos = jnp.sum(onehot * first_vec, axis=-1, keepdims=True) * chunk + rank
    pos_sc[...] = jnp.broadcast_to(pos, pos_sc.shape)
    posr_sc[...] = jnp.broadcast_to(pos, (tm, V7X_LANES)).T[0:8, :]
    moe_sc[...] = jnp.zeros(moe_sc.shape, F32)

    col_id = lax.broadcasted_iota(jnp.int32, (tm, MOE_PAD), 1)
    col_f = col_id.astype(F32)
    row_f = lax.broadcasted_iota(jnp.int32, (chunk, tm), 0).astype(F32)

    def chunk_body(k, carry):
        g = (jnp.where(k >= first_chunk[1], 1, 0) + jnp.where(k >= first_chunk[2], 1, 0)
             + jnp.where(k >= first_chunk[3], 1, 0))
        off = (k * chunk).astype(F32)
        gather = jnp.where(posr_sc[0:1, :] - off == row_f, 1.0, 0.0).astype(BF16)
        scatter = jnp.where((pos_sc[...] - off == col_f) & (col_id < chunk), 1.0, 0.0).astype(BF16)
        hc = jnp.dot(gather, h2_sc[...], preferred_element_type=F32).astype(BF16)
        gc = (jnp.dot(gather, g_sc[0], preferred_element_type=F32)
              + jnp.dot(gather, g_sc[1], preferred_element_type=F32))
        au = jnp.dot(hc, wgu_ref[g], preferred_element_type=F32)
        gw = EXPERTS_PER_GROUP * D_EXPERT
        hh = []
        for e4 in range(EXPERTS_PER_GROUP):
            a = au[:, e4 * D_EXPERT:(e4 + 1) * D_EXPERT]
            u = au[:, gw + e4 * D_EXPERT:gw + (e4 + 1) * D_EXPERT]
            hh.append((_silu(a) * u * gc[:, e4:e4 + 1]).astype(BF16))
        oc = jnp.dot(jnp.concatenate(hh, axis=1), wd_ref[g], preferred_element_type=F32)
        oc = jnp.concatenate([oc.astype(BF16), jnp.zeros((MOE_PAD - chunk, oc.shape[1]), BF16)], axis=0)
        moe_sc[...] += jnp.dot(scatter, oc, preferred_element_type=F32)
        return carry

    lax.fori_loop(0, first_chunk[N_GROUPS], chunk_body, 0)
    x2 = x1 + gt2_ref[0] * moe_sc[...]
    y_ref[0] = (x2 * lax.rsqrt(jnp.mean(x2 * x2, axis=-1, keepdims=True) + NORM_EPS)) * gfin_ref[...]


def _ffn(x, od, rf, mods, weights, *, tm):
    bx, tx, d = x.shape
    gt1, sc2, sh2, gt2 = mods
    beta, w_out_b, g_ffn, w_router, b_router, wgu_b, wd_b, g_fin = weights
    r = gt1.shape[1]
    chunk = min(MOE_CHUNK, tm)
    tri = jnp.asarray(np.tril(np.ones((tm, tm), np.float32), -1), BF16)
    mod_spec = pl.BlockSpec((1, r, d), (lambda b, t: (b, 0, 0)) if r == 1 else (lambda b, t: (b, t, 0)))
    const = lambda shape: pl.BlockSpec(shape, lambda b, t: (0,) * len(shape), pipeline_mode=pl.Buffered(1))
    tok = lambda w: pl.BlockSpec((1, tm, w), lambda b, t: (b, t, 0))
    gw = EXPERTS_PER_GROUP * D_EXPERT
    return pl.pallas_call(
        functools.partial(_ffn_kernel, chunk=chunk),
        grid=(bx, tx // tm),
        in_specs=[tok(d), tok(W_DIFF), tok(W_RET), const((1, d)), const((d, d)), mod_spec, const((1, d)),
                  mod_spec, mod_spec, mod_spec, const((d, 2 * V7X_LANES)), const((1, V7X_LANES)),
                  const((tm, tm)), const((N_GROUPS, d, 2 * gw)), const((N_GROUPS, gw, d)), const((1, d))],
        out_specs=tok(d),
        out_shape=jax.ShapeDtypeStruct((bx, tx, d), F32),
        scratch_shapes=[pltpu.VMEM((tm, d), BF16), pltpu.VMEM((2, tm, V7X_LANES), BF16),
                        pltpu.VMEM((tm, MOE_PAD), F32), pltpu.VMEM((8, tm), F32),
                        pltpu.VMEM((tm, d), F32)],
        compiler_params=pltpu.CompilerParams(
            dimension_semantics=("arbitrary", "arbitrary"), vmem_limit_bytes=VMEM_LIMIT),
        name="ffn",
    )(x, od, rf, beta, w_out_b, gt1, g_ffn, sc2, sh2, gt2, w_router, b_router, tri, wgu_b, wd_b, g_fin)


def kernel(x_prompt, x_sample, cache_k, cache_v, state_ret, page_table, c_prompt, c_sample, w_ada, b_ada,
           norm_mix_g, norm_ffn_g, w_in, lambda_q1, lambda_k1, lambda_q2, lambda_k2, beta_mix, w_out,
           w_group, b_group, w_expert_router, b_expert_router, w_gate_e, w_up_e, w_down_e, final_g):
    assert w_ada.shape[0] == 1, "single-layer stack only"
    bp, tp, d = x_prompt.shape
    bs, ts, _ = x_sample.shape
    n_pages = page_table.shape[1]
    past_len = n_pages * PAGE_SIZE
    n_phys = cache_k.shape[1]

    w_in_b = w_in[0].astype(BF16)
    w_out_b = w_out[0].astype(BF16)
    gw = EXPERTS_PER_GROUP * D_EXPERT

    def by_group(w):
        return jnp.transpose(w.reshape(N_GROUPS, EXPERTS_PER_GROUP, d, D_EXPERT), (0, 2, 1, 3)).reshape(
            N_GROUPS, d, gw)

    wgu_b = jnp.concatenate([by_group(w_gate_e[0]), by_group(w_up_e[0])], axis=-1).astype(BF16)
    wd_b = w_down_e[0].reshape(N_GROUPS, gw, d).astype(BF16)
    w_er = jnp.transpose(w_expert_router[0], (1, 0, 2)).reshape(d, N_EXPERTS)
    pad = V7X_LANES - N_GROUPS - N_EXPERTS
    w_router = jnp.concatenate([w_group[0], w_er, jnp.zeros((d, pad), F32)], axis=1)
    w_router_hi = w_router.astype(BF16)
    w_router = jnp.concatenate([w_router_hi, (w_router - w_router_hi.astype(F32)).astype(BF16)], axis=1)
    b_router = jnp.concatenate([b_group[0], b_expert_router[0].reshape(N_EXPERTS), jnp.zeros((pad,), F32)])[None]
    ffn_weights = (beta_mix, w_out_b, norm_ffn_g, w_router, b_router, wgu_b, wd_b, final_g[None])
    lams = (lambda_q1, lambda_k1, lambda_q2, lambda_k2)

    mod = _adaln(jnp.concatenate([c_prompt, c_sample], axis=0), w_ada[0], b_ada[0])
    mp = [m[:, None, :] for m in jnp.split(mod[:bp], 6, axis=-1)]
    ms = [jnp.repeat(m, ts, axis=0)[None] for m in jnp.split(mod[bp:], 6, axis=-1)]

    tabs_p = _rope_tables(jnp.arange(tp, dtype=F32))
    kp, vp, qT, kb, vT, rf_p, ret_p = _inproj(
        x_prompt, norm_mix_g, mp[1], mp[0], w_in_b, tabs_p, tm=512, transposed=True, fuse_retention=True)
    od_p = _attn_prompt(qT, kb, vT, lams, tq=ATTN_TQ, heads=2)
    y_p = _ffn(x_prompt, od_p, rf_p, (mp[2], mp[4], mp[3], mp[5]), ffn_weights, tm=512)

    rows_s = bs * ts
    tabs_s = _rope_tables(jnp.tile(past_len + jnp.arange(ts, dtype=F32), bs))
    ks, vs, q_s, kb_s, vb_s, qr_s, kr_s, vr_s, gr_s = _inproj(
        x_sample.reshape(1, rows_s, d), norm_mix_g, ms[1], ms[0], w_in_b, tabs_s, tm=rows_s, transposed=False,
        fuse_retention=False)
    sub = jnp.arange(2 * H_DIFF)
    col_sub = jnp.arange(512) // DH_DIFF
    q_rows = jnp.where((sub[:, None, None] == col_sub[None, None, :])[None],
                       q_s.reshape(bs, 1, ts, 512), jnp.zeros((), BF16)).reshape(bs, 2 * H_DIFF * ts, 512)
    pad_keys = lambda a: jnp.pad(a.reshape(bs, ts, 512), ((0, 0), (0, PAGE_SIZE - ts), (0, 0)))
    ck = jnp.transpose(cache_k[0], (0, 2, 3, 1)).reshape(n_phys, 2 * H_DIFF * DH_DIFF, PAGE_SIZE)
    cv = cache_v[0].reshape(n_phys, PAGE_SIZE * H_DIFF, DV_DIFF)
    od_s = _attn_sample(q_rows, pad_keys(kb_s), pad_keys(vb_s), ck, cv, page_table, lams, pages=32, ts=ts)
    chunk_s = 16
    pad_tok = lambda a: jnp.pad(a.reshape(bs, ts, 512), ((0, 0), (0, chunk_s - ts), (0, 0)))
    rf_s, ret_s = _retention(pad_tok(qr_s), pad_tok(kr_s), pad_tok(vr_s), pad_tok(gr_s), state_ret[0],
                             chunk=chunk_s, chunk_len=ts, n_chunks=1)
    y_s = _ffn(x_sample.reshape(1, rows_s, d), od_s.reshape(1, rows_s, 512),
               rf_s[:, :ts].reshape(1, rows_s, 512), (ms[2], ms[4], ms[3], ms[5]), ffn_weights, tm=rows_s)

    return (y_p, y_s.reshape(bs, ts, d),
            kp.reshape(1, bp, tp, 2 * H_DIFF, DH_DIFF), vp.reshape(1, bp, tp, H_DIFF, DV_DIFF), ret_p[None],
            ks.reshape(1, bs, ts, 2 * H_DIFF, DH_DIFF), vs.reshape(1, bs, ts, H_DIFF, DV_DIFF), ret_s[None])
```

```python
import functools
import math

import jax
import jax.numpy as jnp
import numpy as np
from jax import lax
from jax.experimental import pallas as pl
from jax.experimental.pallas import tpu as pltpu

F32 = jnp.float32
BF16 = jnp.bfloat16

V7X_LANES = 128
V7X_VMEM_BYTES = 64 * 1024 * 1024
VMEM_LIMIT = 56 * 1024 * 1024

D_MODEL = 1024
PAGE_SIZE = 128
H_DIFF = 4
DH_DIFF = 64
DV_DIFF = 128
W_DIFF = H_DIFF * DV_DIFF
H_RET = 4
DK_RET = 128
DV_RET = 128
W_RET = H_RET * DV_RET
IN_COLS = 7 * 512
RET_CHUNK = 128
ROPE_THETA = 10000.0
N_GROUPS = 4
EXPERTS_PER_GROUP = 4
N_EXPERTS = 16
D_EXPERT = 256
NORM_EPS = 1e-6
GN_EPS = 1e-5
LAM_INIT = 0.8 - 0.6 * math.exp(-0.3 * 0)
LOG2E = 1.4426950408889634
NEG_BIG = -1e30
LOG_G = tuple(math.log1p(-2.0 ** (-5.0 - h)) for h in range(H_RET))
ROUTER_E0 = N_GROUPS


def _silu(x):
    return x * jax.nn.sigmoid(x)


def _adaln_kernel(c_ref, w_ref, b_ref, o_ref):
    s = _silu(c_ref[...])
    o_ref[...] = jnp.dot(s, w_ref[...], preferred_element_type=F32,
                         precision=lax.Precision.HIGHEST) + b_ref[...]


def _adaln(c, w_ada, b_ada):
    n, d = c.shape
    cols = w_ada.shape[1]
    bn = 1024
    return pl.pallas_call(
        _adaln_kernel,
        grid=(cols // bn,),
        in_specs=[pl.BlockSpec((n, d), lambda j: (0, 0)),
                  pl.BlockSpec((d, bn), lambda j: (0, j)),
                  pl.BlockSpec((1, bn), lambda j: (0, j))],
        out_specs=pl.BlockSpec((n, bn), lambda j: (0, j)),
        out_shape=jax.ShapeDtypeStruct((n, cols), F32),
        name="adaln",
    )(c, w_ada, b_ada.reshape(1, cols))


def _rope64(z, cos, sin_signed, first_half):
    rot = jnp.where(first_half, pltpu.roll(z, 96, 1), pltpu.roll(z, 32, 1))
    return z * cos + rot * sin_signed


def _rope128(z, cos, sin_signed):
    return z * cos + pltpu.roll(z, 64, 1) * sin_signed


def _inproj_kernel(x_ref, g_ref, sc_ref, sh_ref, w_ref, cd_ref, sd_ref, cr_ref, sr_ref,
                   kout_ref, vout_ref, q_ref, kb_ref, v_ref, *rest, transposed, fuse_retention):
    tm = x_ref.shape[1]
    if fuse_retention:
        rf_ref, rout_ref, state = rest
        t = pl.program_id(1)

        @pl.when(t == 0)
        def _():
            state[...] = jnp.zeros(state.shape, F32)
    else:
        qr_ref, kr_ref, vr_ref, gr_ref = rest
    x = x_ref[0]
    y = x * lax.rsqrt(jnp.mean(x * x, axis=-1, keepdims=True) + NORM_EPS)
    h = (y * g_ref[...]) * (1.0 + sc_ref[0]) + sh_ref[0]
    hb = h.astype(BF16)

    def proj(section):
        return jnp.dot(hb, w_ref[:, section * 512:(section + 1) * 512], preferred_element_type=F32)

    zq, zk, zv, zqr, zkr, zvr, zgr = [proj(i) for i in range(7)]

    cd, sd, cr, sr = cd_ref[...], sd_ref[...], cr_ref[...], sr_ref[...]
    lane = lax.broadcasted_iota(jnp.int32, (tm, V7X_LANES), 1)
    first_half = (lane % DH_DIFF) < (DH_DIFF // 2)
    q_scale = (DH_DIFF ** -0.5) * LOG2E
    for j in range(4):
        cols = slice(j * V7X_LANES, (j + 1) * V7X_LANES)
        q = _rope64(zq[:, cols], cd, sd, first_half) * q_scale
        k = _rope64(zk[:, cols], cd, sd, first_half)
        v = zv[:, cols]
        kout_ref[0, :, cols] = k
        vout_ref[0, pl.ds(j, tm, stride=H_DIFF), :] = v
        kb_ref[0, :, cols] = k.astype(BF16)
        if transposed:
            q_ref[0, cols, :] = q.T.astype(BF16)
            v_ref[0, cols, :] = v.T.astype(BF16)
        else:
            q_ref[0, :, cols] = q.astype(BF16)
            v_ref[0, :, cols] = v.astype(BF16)
        qr = _rope128(zqr[:, cols], cr, sr).astype(BF16)
        kr = (_rope128(zkr[:, cols], cr, sr) * (DK_RET ** -0.5)).astype(BF16)
        vr = zvr[:, cols].astype(BF16)
        gr = zgr[:, cols].astype(BF16)
        if fuse_retention:
            operands = (qr, kr, vr, gr)

            def get(i, rows, operands=operands):
                return operands[i][rows]

            def put(rows, val, cols=cols):
                rf_ref[0, rows, cols] = val.astype(rf_ref.dtype)

            for unit in _retention_units(get, state, j, put, chunk=RET_CHUNK, chunk_len=RET_CHUNK,
                                         n_chunks=tm // RET_CHUNK):
                unit()
        else:
            qr_ref[0, :, cols] = qr
            kr_ref[0, :, cols] = kr
            vr_ref[0, :, cols] = vr
            gr_ref[0, :, cols] = gr

    if fuse_retention:
        @pl.when(t == pl.num_programs(1) - 1)
        def _():
            rout_ref[0] = state[...]


def _mod_spec(mod, d, col):
    r = mod.shape[1]
    if r == 1:
        return pl.BlockSpec((1, 1, d), lambda b, t: (b, 0, col))
    return pl.BlockSpec((1, r, d), lambda b, t: (b, t, col))


MOD_SH1, MOD_SC1, MOD_GT1, MOD_SH2, MOD_SC2, MOD_GT2 = range(6)


def _inproj(x, g, mod, w_in_b, tables, *, tm, transposed, fuse_retention):
    bx, tx, d = x.shape
    nt = tx // tm
    tab_spec = pl.BlockSpec((tm, V7X_LANES), lambda b, t: (t, 0))
    nat = lambda dt: jax.ShapeDtypeStruct((bx, tx, 512), dt)
    nat_spec = pl.BlockSpec((1, tm, 512), lambda b, t: (b, t, 0))
    if transposed:
        tr = jax.ShapeDtypeStruct((bx, 512, tx), BF16)
        tr_spec = pl.BlockSpec((1, 512, tm), lambda b, t: (b, 0, t))
    else:
        tr, tr_spec = nat(BF16), nat_spec
    vrows = jax.ShapeDtypeStruct((bx, tx * H_DIFF, DV_DIFF), F32)
    vrows_spec = pl.BlockSpec((1, tm * H_DIFF, DV_DIFF), lambda b, t: (b, t, 0))
    out_shape = [nat(F32), vrows, tr, nat(BF16), tr]
    out_specs = [nat_spec, vrows_spec, tr_spec, nat_spec, tr_spec]
    scratch = []
    if fuse_retention:
        out_shape += [nat(BF16), jax.ShapeDtypeStruct((bx, H_RET, DK_RET, DV_RET), F32)]
        out_specs += [nat_spec, pl.BlockSpec((1, H_RET, DK_RET, DV_RET), lambda b, t: (b, 0, 0, 0))]
        scratch = [pltpu.VMEM((H_RET, DK_RET, DV_RET), F32)]
    else:
        out_shape += [nat(BF16)] * 4
        out_specs += [nat_spec] * 4
    return pl.pallas_call(
        functools.partial(_inproj_kernel, transposed=transposed, fuse_retention=fuse_retention),
        grid=(bx, nt),
        scratch_shapes=scratch,
        in_specs=[pl.BlockSpec((1, tm, d), lambda b, t: (b, t, 0)),
                  pl.BlockSpec((1, d), lambda b, t: (0, 0)),
                  _mod_spec(mod, d, MOD_SC1), _mod_spec(mod, d, MOD_SH1),
                  pl.BlockSpec((d, IN_COLS), lambda b, t: (0, 0)),
                  tab_spec, tab_spec, tab_spec, tab_spec],
        out_specs=out_specs,
        out_shape=out_shape,
        compiler_params=pltpu.CompilerParams(
            dimension_semantics=("arbitrary", "arbitrary"), vmem_limit_bytes=VMEM_LIMIT),
        name="inproj",
    )(x, g, mod, mod, w_in_b, *tables)


def _rope_tables(pos):
    lane = np.arange(V7X_LANES)
    pos = np.asarray(pos, np.float64)

    def one(d):
        half = d // 2
        inv = np.power(ROPE_THETA, -(2.0 / d) * np.arange(half, dtype=np.float64))
        ang = pos[:, None] * inv[None, :]
        idx = lane % half
        sign = np.where((lane % d) < half, -1.0, 1.0)
        return (jnp.asarray(np.cos(ang)[:, idx], F32), jnp.asarray(np.sin(ang)[:, idx] * sign[None, :], F32))

    cd, sd = one(DH_DIFF)
    cr, sr = one(DK_RET)
    return cd, sd, cr, sr


def _diff_lambda(lq1_ref, lk1_ref, lq2_ref, lk2_ref):
    e1 = jnp.exp(jnp.sum(lq1_ref[...] * lk1_ref[...], axis=-1, keepdims=True))
    e2 = jnp.exp(jnp.sum(lq2_ref[...] * lk2_ref[...], axis=-1, keepdims=True))
    return e1 - e2 + LAM_INIT


ATTN_SW = 256
ATTN_KB = 256
ATTN_TQ = 2 * ATTN_KB
ATTN_ONES_ROWS = 16


def _attn_kernel(qT_ref, k_ref, vT_ref, lq1_ref, lk1_ref, lq2_ref, lk2_ref, o_ref,
                 m_sc, l_sc, acc_sc, s_buf, p_buf, a_buf, *, tq):
    sw, kb = ATTN_SW, ATTN_KB
    assert tq == 2 * kb and tq % sw == 0
    ns = tq // sw
    nh = qT_ref.shape[1] // V7X_LANES
    qi = pl.program_id(2)
    row = lax.broadcasted_iota(jnp.int32, (V7X_LANES, tq), 0)
    zero = jnp.zeros((V7X_LANES, tq), BF16)
    q_sub = []
    for hh in range(nh):
        qT = qT_ref[0, hh * V7X_LANES:(hh + 1) * V7X_LANES, :]
        q_sub.append((jnp.where(row < DH_DIFF, qT, zero), jnp.where(row >= DH_DIFF, qT, zero)))
    m_sc[...] = jnp.full(m_sc.shape, NEG_BIG, F32)
    l_sc[...] = jnp.zeros(l_sc.shape, F32)
    acc_sc[...] = jnp.zeros(acc_sc.shape, F32)
    chains = [(hh, a, st) for hh in range(nh) for a in range(2) for st in range(ns)]
    chain_id = lambda hh, a, st: (hh * 2 + a) * ns + st
    head_lanes = lambda hh: slice(hh * V7X_LANES, (hh + 1) * V7X_LANES)

    def scores(slot, key_base, diag_step):
        key0 = pl.multiple_of(key_base, kb)
        for hh, a, st in chains:
            kblk = k_ref[0, pl.ds(key0, kb), head_lanes(hh)]
            s = jnp.dot(kblk, q_sub[hh][a][:, st * sw:(st + 1) * sw], preferred_element_type=F32)
            if diag_step is not None:
                off = diag_step * kb - st * sw
                if off + kb - 1 > 0:
                    kpos = off + lax.broadcasted_iota(jnp.int32, (kb, sw), 0)
                    qpos = lax.broadcasted_iota(jnp.int32, (kb, sw), 1)
                    s = jnp.where(kpos <= qpos, s, -jnp.inf)
            s_buf[slot, chain_id(hh, a, st)] = s

    def softmax(slot):
        for hh, a, st in chains:
            c = chain_id(hh, a, st)
            m_old = m_sc[c]
            m_new = jnp.maximum(m_old, jnp.max(s_buf[slot, c], axis=0, keepdims=True))
            alpha = jnp.exp2(m_old - m_new)
            p = jnp.exp2(s_buf[slot, c] - m_new)
            m_sc[c] = m_new
            a_buf[slot, c] = alpha
            p_buf[slot, c] = p.astype(BF16)

    def values(slot, key_base):
        key0 = pl.multiple_of(key_base, kb)
        ones = jnp.ones((ATTN_ONES_ROWS, kb), BF16)
        for hh, a, st in chains:
            c = chain_id(hh, a, st)
            vblk = jnp.concatenate([vT_ref[0, head_lanes(hh), pl.ds(key0, kb)], ones], axis=0)
            pv = jnp.dot(vblk, p_buf[slot, c], preferred_element_type=F32)
            alpha = a_buf[slot, c]
            acc_sc[c] = acc_sc[c] * alpha + pv[:DV_DIFF]
            l_sc[c] = l_sc[c] * alpha + pv[DV_DIFF:DV_DIFF + 1]

    diag_base = [qi * tq + d * kb for d in range(2)]
    scores(0, diag_base[0], 0)
    scores(1, diag_base[1], 1)
    softmax(0)

    def trip_pair(u, carry):
        for par_t in range(2):
            full_base = (2 * u + par_t) * kb
            lag_base = jnp.where(u == 0, diag_base[par_t], full_base - 2 * kb)
            scores(par_t, full_base, None)
            softmax(1 - par_t)
            values(par_t, lag_base)
        return carry

    lax.fori_loop(0, qi, trip_pair, 0)
    last = [jnp.where(qi == 0, diag_base[d], (2 * qi - 2 + d) * kb) for d in range(2)]
    softmax(1)
    values(0, last[0])
    values(1, last[1])

    lam = _diff_lambda(lq1_ref, lk1_ref, lq2_ref, lk2_ref)
    for hh in range(nh):
        for st in range(ns):
            outs = [acc_sc[chain_id(hh, a, st)] / l_sc[chain_id(hh, a, st)] for a in range(2)]
            o = outs[0] - lam * outs[1]
            of = o * lax.rsqrt(jnp.mean(o * o, axis=0, keepdims=True) + NORM_EPS) * (1.0 - LAM_INIT)
            o_ref[0, st * sw:(st + 1) * sw, head_lanes(hh)] = of.T.astype(o_ref.dtype)


def _attn_prompt(qT, kb, vT, lams, *, tq, heads):
    b, _, t = qT.shape
    lam_spec = pl.BlockSpec((1, DH_DIFF), lambda b_, h, i: (0, 0))
    n_chains = heads * 2 * (tq // ATTN_SW)
    hw = heads * V7X_LANES
    return pl.pallas_call(
        functools.partial(_attn_kernel, tq=tq),
        grid=(b, H_DIFF // heads, t // tq),
        in_specs=[pl.BlockSpec((1, hw, tq), lambda b_, h, i: (b_, h, i)),
                  pl.BlockSpec((1, t, hw), lambda b_, h, i: (b_, 0, h)),
                  pl.BlockSpec((1, hw, t), lambda b_, h, i: (b_, h, 0)),
                  lam_spec, lam_spec, lam_spec, lam_spec],
        out_specs=pl.BlockSpec((1, tq, hw), lambda b_, h, i: (b_, i, h)),
        out_shape=jax.ShapeDtypeStruct((b, t, W_DIFF), BF16),
        scratch_shapes=[pltpu.VMEM((n_chains, 1, ATTN_SW), F32), pltpu.VMEM((n_chains, 1, ATTN_SW), F32),
                        pltpu.VMEM((n_chains, DV_DIFF, ATTN_SW), F32),
                        pltpu.VMEM((2, n_chains, ATTN_KB, ATTN_SW), F32),
                        pltpu.VMEM((2, n_chains, ATTN_KB, ATTN_SW), BF16),
                        pltpu.VMEM((2, n_chains, 1, ATTN_SW), F32)],
        compiler_params=pltpu.CompilerParams(
            dimension_semantics=("arbitrary", "arbitrary", "arbitrary"), vmem_limit_bytes=VMEM_LIMIT),
        name="attn_prompt",
    )(qT, kb, vT, *lams)


def _decode_kernel(pt_ref, q_ref, kn_ref, vn_ref, lq1_ref, lk1_ref, lq2_ref, lk2_ref, *rest, pages, ts):
    k_refs, v_refs = rest[:pages], rest[pages:2 * pages]
    o_ref, m_sc, l_sc, acc_sc = rest[2 * pages:]
    s_id = pl.program_id(1)
    q = q_ref[0]
    hr = 2 * ts

    @pl.when(s_id == 0)
    def _():
        m_sc[...] = jnp.full(m_sc.shape, NEG_BIG, F32)
        l_sc[...] = jnp.zeros(l_sc.shape, F32)
        acc_sc[...] = jnp.zeros(acc_sc.shape, F32)

    def update(s, v_heads):
        m_old = m_sc[:, 0:1]
        m_new = jnp.maximum(m_old, jnp.max(s, axis=1, keepdims=True))
        alpha = jnp.exp2(m_old - m_new)
        p = jnp.exp2(s - m_new)
        l_new = alpha * l_sc[:, 0:1] + jnp.sum(p, axis=1, keepdims=True)
        pb = p.astype(BF16)
        for h in range(H_DIFF):
            rows = slice(h * hr, (h + 1) * hr)
            pv = jnp.dot(pb[rows, :], v_heads[h], preferred_element_type=F32)
            acc_sc[rows, :] = acc_sc[rows, :] * alpha[rows, :] + pv
        m_sc[...] = jnp.broadcast_to(m_new, m_sc.shape)
        l_sc[...] = jnp.broadcast_to(l_new, l_sc.shape)

    kT = jnp.concatenate([kr[0].astype(BF16) for kr in k_refs], axis=1)
    s_past = jnp.dot(q, kT, preferred_element_type=F32)
    update(s_past, [jnp.concatenate([vr[0, pl.ds(h, PAGE_SIZE, stride=H_DIFF), :].astype(BF16)
                                     for vr in v_refs], axis=0) for h in range(H_DIFF)])

    @pl.when(s_id == pl.num_programs(1) - 1)
    def _():
        s_new = lax.dot_general(q, kn_ref[0], (((1,), (1,)), ((), ())), preferred_element_type=F32)
        key = lax.broadcasted_iota(jnp.int32, s_new.shape, 1)
        tok = lax.broadcasted_iota(jnp.int32, s_new.shape, 0) % ts
        s_new = jnp.where(key <= tok, s_new, NEG_BIG)
        update(s_new, [vn_ref[0, :, h * DV_DIFF:(h + 1) * DV_DIFF] for h in range(H_DIFF)])
        lam = _diff_lambda(lq1_ref, lk1_ref, lq2_ref, lk2_ref)
        on = acc_sc[...] / l_sc[:, 0:1]
        for h in range(H_DIFF):
            o = on[h * hr:h * hr + ts, :] - lam * on[h * hr + ts:(h + 1) * hr, :]
            of = o * lax.rsqrt(jnp.mean(o * o, axis=-1, keepdims=True) + NORM_EPS) * (1.0 - LAM_INIT)
            o_ref[0, :, h * DV_DIFF:(h + 1) * DV_DIFF] = of


def _attn_sample(q_rows, k_new, v_new, cache_k, cache_v, page_table, lams, *, pages, ts):
    bs, n_pages = page_table.shape
    rows = q_rows.shape[1]
    steps = n_pages // pages
    lam_spec = pl.BlockSpec((1, DH_DIFF), lambda b, s, pt: (0, 0))

    assert 2 * ts == 8, "one head's query rows must fill one sublane tile"

    def page_spec(i):
        return pl.BlockSpec((1, 512, V7X_LANES), lambda b, s, pt: (pt[b, s * pages + i], 0, 0))

    per_b = lambda r: pl.BlockSpec((1, r, 512), lambda b, s, pt: (b, 0, 0))
    grid_spec = pltpu.PrefetchScalarGridSpec(
        num_scalar_prefetch=1,
        grid=(bs, steps),
        in_specs=[per_b(rows), per_b(PAGE_SIZE), per_b(PAGE_SIZE), lam_spec, lam_spec, lam_spec, lam_spec]
                 + [page_spec(i) for i in range(pages)] + [page_spec(i) for i in range(pages)],
        out_specs=per_b(ts),
        scratch_shapes=[pltpu.VMEM((rows, V7X_LANES), F32), pltpu.VMEM((rows, V7X_LANES), F32),
                        pltpu.VMEM((rows, DV_DIFF), F32)],
    )
    return pl.pallas_call(
        functools.partial(_decode_kernel, pages=pages, ts=ts),
        grid_spec=grid_spec,
        out_shape=jax.ShapeDtypeStruct((bs, ts, W_DIFF), F32),
        compiler_params=pltpu.CompilerParams(
            dimension_semantics=("arbitrary", "arbitrary"), vmem_limit_bytes=VMEM_LIMIT),
        name="attn_sample",
    )(page_table, q_rows, k_new, v_new, *lams, *([cache_k] * pages), *([cache_v] * pages))


def _retention_units(get, state, h, put, *, chunk, chunk_len, n_chunks):
    ii = lax.broadcasted_iota(jnp.int32, (chunk, chunk), 0)
    jj = lax.broadcasted_iota(jnp.int32, (chunk, chunk), 1)
    dif = (ii - jj).astype(F32)
    tok = lax.broadcasted_iota(jnp.int32, (chunk, DK_RET), 0).astype(F32)
    decay = jnp.where(dif >= 0, jnp.exp(jnp.maximum(dif, 0.0) * LOG_G[h]), 0.0)
    q_dec = jnp.exp((tok + 1.0) * LOG_G[h])
    k_dec = jnp.exp((chunk_len - 1.0 - tok) * LOG_G[h])
    chunk_dec = math.exp(chunk_len * LOG_G[h])

    def unit(c):
        rows = slice(c * chunk, (c + 1) * chunk)
        qc, kc, vc = get(0, rows), get(1, rows), get(2, rows)
        r_prev = state[h]
        s = lax.dot_general(qc, kc, (((1,), (1,)), ((), ())), preferred_element_type=F32) * decay
        o = (jnp.dot(s.astype(qc.dtype), vc, preferred_element_type=F32)
             + jnp.dot(qc, r_prev.astype(qc.dtype), preferred_element_type=F32) * q_dec)
        vd = (vc.astype(F32) * k_dec).astype(qc.dtype)
        kv = lax.dot_general(kc, vd, (((0,), (0,)), ((), ())), preferred_element_type=F32)
        state[h] = r_prev * chunk_dec + kv
        mu = jnp.mean(o, axis=-1, keepdims=True)
        var = jnp.mean(jnp.square(o - mu), axis=-1, keepdims=True)
        nrm = (o - mu) * lax.rsqrt(var + GN_EPS)
        put(rows, nrm * _silu(get(3, rows).astype(F32)))

    return [functools.partial(unit, c) for c in range(n_chunks)]


def _retention_kernel(q_ref, k_ref, v_ref, g_ref, r0_ref, o_ref, rout_ref, state, *, chunk, chunk_len, n_chunks):
    t = pl.program_id(1)
    refs = (q_ref, k_ref, v_ref, g_ref)

    @pl.when(t == 0)
    def _():
        state[...] = r0_ref[...]

    for bi in range(q_ref.shape[0]):
        for h in range(H_RET):
            cols = slice(h * DK_RET, (h + 1) * DK_RET)

            def get(i, rows, bi=bi, cols=cols):
                return refs[i][bi, rows, cols]

            def put(rows, val, bi=bi, cols=cols):
                o_ref[bi, rows, cols] = val.astype(o_ref.dtype)

            for unit in _retention_units(get, state.at[bi], h, put, chunk=chunk, chunk_len=chunk_len,
                                         n_chunks=n_chunks):
                unit()

    @pl.when(t == pl.num_programs(1) - 1)
    def _():
        rout_ref[...] = state[...]


def _retention(qr, kr, vr, gr, r0, *, chunk, chunk_len, n_chunks, seqs):
    b, t, _ = qr.shape
    tr = chunk * n_chunks
    tok_spec = pl.BlockSpec((seqs, tr, 512), lambda b_, i: (b_, i, 0))
    st_spec = pl.BlockSpec((seqs, H_RET, DK_RET, DV_RET), lambda b_, i: (b_, 0, 0, 0))
    return pl.pallas_call(
        functools.partial(_retention_kernel, chunk=chunk, chunk_len=chunk_len, n_chunks=n_chunks),
        grid=(b // seqs, t // tr),
        in_specs=[tok_spec] * 4 + [st_spec],
        out_specs=[tok_spec, st_spec],
        out_shape=[jax.ShapeDtypeStruct((b, t, W_RET), BF16),
                   jax.ShapeDtypeStruct((b, H_RET, DK_RET, DV_RET), F32)],
        scratch_shapes=[pltpu.VMEM((seqs, H_RET, DK_RET, DV_RET), F32)],
        compiler_params=pltpu.CompilerParams(
            dimension_semantics=("arbitrary", "arbitrary"), vmem_limit_bytes=VMEM_LIMIT),
        name="retention",
    )(qr, kr, vr, gr, r0)


def _route(z):
    lane = lax.broadcasted_iota(jnp.int32, z.shape, 1).astype(F32)
    big = 1e9
    neg = -jnp.inf
    is_g = lane < N_GROUPS
    glog = jnp.where(is_g, z, neg)
    gmax = jnp.max(glog, axis=-1, keepdims=True)
    gsel = jnp.min(jnp.where(glog == gmax, lane, big), axis=-1, keepdims=True)
    psum = jnp.sum(jnp.where(is_g, jnp.exp(z - gmax), 0.0), axis=-1, keepdims=True)
    pg_sel = 1.0 / psum
    rel = jnp.zeros_like(z)
    for g in range(N_GROUPS):
        first = ROUTER_E0 + g * EXPERTS_PER_GROUP
        rel = jnp.where(gsel == float(g), pltpu.roll(z, V7X_LANES - first, 1), rel)
    elog = jnp.where(lane < EXPERTS_PER_GROUP, rel, neg)
    v1 = jnp.max(elog, axis=-1, keepdims=True)
    i1 = jnp.min(jnp.where(elog == v1, lane, big), axis=-1, keepdims=True)
    elog2 = jnp.where(lane == i1, neg, elog)
    v2 = jnp.max(elog2, axis=-1, keepdims=True)
    i2 = jnp.min(jnp.where(elog2 == v2, lane, big), axis=-1, keepdims=True)
    e2 = jnp.exp(v2 - v1)
    den = 1.0 + e2
    w1 = (1.0 / den) * pg_sel
    w2 = (e2 / den) * pg_sel
    return gsel, jnp.where(lane == i1, w1, jnp.where(lane == i2, w2, 0.0))


MOE_CHUNK = 160
MOE_PAD = 256


def _ffn_kernel(x_ref, od_ref, rf_ref, beta_ref, wout_ref, gt1_ref, gffn_ref, sc2_ref, sh2_ref, gt2_ref,
                wr_ref, br_ref, tri_ref, wgu_ref, wd_ref, gfin_ref, y_ref,
                h2_sc, g_sc, pos_sc, posr_sc, moe_sc, *, chunk):
    tm = x_ref.shape[1]
    beta = beta_ref[...]
    yd = (od_ref[0].astype(F32) * beta[:, :W_DIFF]).astype(BF16)
    yr = (rf_ref[0].astype(F32) * beta[:, W_DIFF:]).astype(BF16)
    mix = (jnp.dot(yd, wout_ref[:W_DIFF, :], preferred_element_type=F32)
           + jnp.dot(yr, wout_ref[W_DIFF:, :], preferred_element_type=F32))
    x1 = x_ref[0] + gt1_ref[0] * mix
    n2 = x1 * lax.rsqrt(jnp.mean(x1 * x1, axis=-1, keepdims=True) + NORM_EPS)
    h2 = (n2 * gffn_ref[...]) * (1.0 + sc2_ref[0]) + sh2_ref[0]
    h_hi = h2.astype(BF16)
    h_lo = (h2 - h_hi.astype(F32)).astype(BF16)
    zz = jnp.dot(h_hi, wr_ref[...], preferred_element_type=F32)
    z = (zz[:, :V7X_LANES] + zz[:, V7X_LANES:]
         + jnp.dot(h_lo, wr_ref[:, :V7X_LANES], preferred_element_type=F32) + br_ref[...])
    gsel, gates = _route(z)
    h2_sc[...] = h_hi
    g_hi = gates.astype(BF16)
    g_sc[0] = g_hi
    g_sc[1] = (gates - g_hi.astype(F32)).astype(BF16)

    lane = lax.broadcasted_iota(jnp.int32, (tm, V7X_LANES), 1).astype(F32)
    lane1 = lax.broadcasted_iota(jnp.int32, (1, V7X_LANES), 1)
    onehot = jnp.where(lane == gsel, 1.0, 0.0)
    earlier = jnp.dot(tri_ref[...], onehot.astype(BF16), preferred_element_type=F32)
    rank = jnp.sum(onehot * earlier, axis=-1, keepdims=True)
    cnt = jnp.sum(onehot, axis=0, keepdims=True)
    n_chunks = jnp.zeros_like(cnt)
    for m in range(-(-tm // chunk)):
        n_chunks = n_chunks + jnp.where(cnt > m * chunk, 1.0, 0.0)
    first_chunk = [jnp.int32(0)]
    for g in range(N_GROUPS):
        first_chunk.append(first_chunk[-1] + n_chunks[0, g].astype(jnp.int32))
    first_vec = jnp.zeros((1, V7X_LANES), F32)
    for g in range(N_GROUPS):
        first_vec = jnp.where(lane1 == g, first_chunk[g].astype(F32), first_vec)
    pos = jnp.sum(onehot * first_vec, axis=-1, keepdims=True) * chunk + rank
    pos_sc[...] = jnp.broadcast_to(pos, pos_sc.shape)
    posr_sc[...] = jnp.broadcast_to(pos, (tm, V7X_LANES)).T[0:8, :]
    moe_sc[...] = jnp.zeros(moe_sc.shape, F32)

    col_id = lax.broadcasted_iota(jnp.int32, (tm, MOE_PAD), 1)
    col_f = col_id.astype(F32)
    row_f = lax.broadcasted_iota(jnp.int32, (chunk, tm), 0).astype(F32)

    def chunk_body(k, carry):
        g = (jnp.where(k >= first_chunk[1], 1, 0) + jnp.where(k >= first_chunk[2], 1, 0)
             + jnp.where(k >= first_chunk[3], 1, 0))
        off = (k * chunk).astype(F32)
        gather = jnp.where(posr_sc[0:1, :] - off == row_f, 1.0, 0.0).astype(BF16)
        scatter = jnp.where((pos_sc[...] - off == col_f) & (col_id < chunk), 1.0, 0.0).astype(BF16)
        hc = jnp.dot(gather, h2_sc[...], preferred_element_type=F32).astype(BF16)
        gc = (jnp.dot(gather, g_sc[0], preferred_element_type=F32)
              + jnp.dot(gather, g_sc[1], preferred_element_type=F32))
        hh = []
        for e4 in range(EXPERTS_PER_GROUP):
            au = jnp.dot(hc, wgu_ref[g * EXPERTS_PER_GROUP + e4], preferred_element_type=F32)
            a, u = au[:, :D_EXPERT], au[:, D_EXPERT:]
            hh.append((_silu(a) * u * gc[:, e4:e4 + 1]).astype(BF16))
        oc = jnp.dot(jnp.concatenate(hh, axis=1), wd_ref[g], preferred_element_type=F32)
        oc = jnp.concatenate([oc.astype(BF16), jnp.zeros((MOE_PAD - chunk, oc.shape[1]), BF16)], axis=0)
        moe_sc[...] += jnp.dot(scatter, oc, preferred_element_type=F32)
        return carry

    lax.fori_loop(0, first_chunk[N_GROUPS], chunk_body, 0)
    x2 = x1 + gt2_ref[0] * moe_sc[...]
    y_ref[0] = (x2 * lax.rsqrt(jnp.mean(x2 * x2, axis=-1, keepdims=True) + NORM_EPS)) * gfin_ref[...]


def _ffn(x, od, rf, mod, weights, *, tm):
    bx, tx, d = x.shape
    beta, w_out_b, g_ffn, w_router, b_router, wgu_b, wd_b, g_fin = weights
    chunk = min(MOE_CHUNK, tm)
    tri = jnp.asarray(np.tril(np.ones((tm, tm), np.float32), -1), BF16)
    const = lambda shape: pl.BlockSpec(shape, lambda b, t: (0,) * len(shape), pipeline_mode=pl.Buffered(1))
    tok = lambda w: pl.BlockSpec((1, tm, w), lambda b, t: (b, t, 0))
    gw = EXPERTS_PER_GROUP * D_EXPERT
    return pl.pallas_call(
        functools.partial(_ffn_kernel, chunk=chunk),
        grid=(bx, tx // tm),
        in_specs=[tok(d), tok(W_DIFF), tok(W_RET), const((1, d)), const((d, d)), _mod_spec(mod, d, MOD_GT1),
                  const((1, d)), _mod_spec(mod, d, MOD_SC2), _mod_spec(mod, d, MOD_SH2), _mod_spec(mod, d, MOD_GT2),
                  const((d, 2 * V7X_LANES)), const((1, V7X_LANES)),
                  const((tm, tm)), const((N_EXPERTS, d, 2 * D_EXPERT)), const((N_GROUPS, gw, d)), const((1, d))],
        out_specs=tok(d),
        out_shape=jax.ShapeDtypeStruct((bx, tx, d), F32),
        scratch_shapes=[pltpu.VMEM((tm, d), BF16), pltpu.VMEM((2, tm, V7X_LANES), BF16),
                        pltpu.VMEM((tm, MOE_PAD), F32), pltpu.VMEM((8, tm), F32),
                        pltpu.VMEM((tm, d), F32)],
        compiler_params=pltpu.CompilerParams(
            dimension_semantics=("arbitrary", "arbitrary"), vmem_limit_bytes=VMEM_LIMIT),
        name="ffn",
    )(x, od, rf, beta, w_out_b, mod, g_ffn, mod, mod, mod, w_router, b_router, tri, wgu_b, wd_b, g_fin)


def kernel(x_prompt, x_sample, cache_k, cache_v, state_ret, page_table, c_prompt, c_sample, w_ada, b_ada,
           norm_mix_g, norm_ffn_g, w_in, lambda_q1, lambda_k1, lambda_q2, lambda_k2, beta_mix, w_out,
           w_group, b_group, w_expert_router, b_expert_router, w_gate_e, w_up_e, w_down_e, final_g):
    assert w_ada.shape[0] == 1, "single-layer stack only"
    bp, tp, d = x_prompt.shape
    bs, ts, _ = x_sample.shape
    n_pages = page_table.shape[1]
    past_len = n_pages * PAGE_SIZE
    n_phys = cache_k.shape[1]

    w_in_b = w_in[0].astype(BF16)
    w_out_b = w_out[0].astype(BF16)
    wgu_b = jnp.concatenate([w_gate_e[0], w_up_e[0]], axis=-1).astype(BF16)
    wd_b = w_down_e[0].reshape(N_GROUPS, EXPERTS_PER_GROUP * D_EXPERT, d).astype(BF16)
    w_er = jnp.transpose(w_expert_router[0], (1, 0, 2)).reshape(d, N_EXPERTS)
    pad = V7X_LANES - N_GROUPS - N_EXPERTS
    w_router = jnp.concatenate([w_group[0], w_er, jnp.zeros((d, pad), F32)], axis=1)
    w_router_hi = w_router.astype(BF16)
    w_router = jnp.concatenate([w_router_hi, (w_router - w_router_hi.astype(F32)).astype(BF16)], axis=1)
    b_router = jnp.concatenate([b_group[0], b_expert_router[0].reshape(N_EXPERTS), jnp.zeros((pad,), F32)])[None]
    ffn_weights = (beta_mix, w_out_b, norm_ffn_g, w_router, b_router, wgu_b, wd_b, final_g[None])
    lams = (lambda_q1, lambda_k1, lambda_q2, lambda_k2)

    mod = _adaln(jnp.concatenate([c_prompt, c_sample], axis=0), w_ada[0], b_ada[0])
    mod_p = mod[:bp, None, :]
    mod_s = jnp.repeat(mod[bp:], ts, axis=0)[None]

    tabs_p = _rope_tables(np.arange(tp))
    kp, vp, qT, kb, vT, rf_p, ret_p = _inproj(
        x_prompt, norm_mix_g, mod_p, w_in_b, tabs_p, tm=512, transposed=True, fuse_retention=True)
    od_p = _attn_prompt(qT, kb, vT, lams, tq=ATTN_TQ, heads=2)
    y_p = _ffn(x_prompt, od_p, rf_p, mod_p, ffn_weights, tm=512)

    rows_s = bs * ts
    tabs_s = _rope_tables(np.tile(past_len + np.arange(ts), bs))
    ks, vs, q_s, kb_s, vb_s, qr_s, kr_s, vr_s, gr_s = _inproj(
        x_sample.reshape(1, rows_s, d), norm_mix_g, mod_s, w_in_b, tabs_s, tm=rows_s, transposed=False,
        fuse_retention=False)
    sub = jnp.arange(2 * H_DIFF)
    col_sub = jnp.arange(512) // DH_DIFF
    q_rows = jnp.where((sub[:, None, None] == col_sub[None, None, :])[None],
                       q_s.reshape(bs, 1, ts, 512), jnp.zeros((), BF16)).reshape(bs, 2 * H_DIFF * ts, 512)
    pad_keys = lambda a: jnp.pad(a.reshape(bs, ts, 512), ((0, 0), (0, PAGE_SIZE - ts), (0, 0)))
    ck = jnp.transpose(cache_k[0], (0, 2, 3, 1)).reshape(n_phys, 2 * H_DIFF * DH_DIFF, PAGE_SIZE)
    cv = cache_v[0].reshape(n_phys, PAGE_SIZE * H_DIFF, DV_DIFF)
    od_s = _attn_sample(q_rows, pad_keys(kb_s), pad_keys(vb_s), ck, cv, page_table, lams, pages=32, ts=ts)
    chunk_s = 16
    pad_tok = lambda a: jnp.pad(a.reshape(bs, ts, 512), ((0, 0), (0, chunk_s - ts), (0, 0)))
    rf_s, ret_s = _retention(pad_tok(qr_s), pad_tok(kr_s), pad_tok(vr_s), pad_tok(gr_s), state_ret[0],
                             chunk=chunk_s, chunk_len=ts, n_chunks=1, seqs=8)
    y_s = _ffn(x_sample.reshape(1, rows_s, d), od_s.reshape(1, rows_s, 512),
               rf_s[:, :ts].reshape(1, rows_s, 512), mod_s, ffn_weights, tm=rows_s)

    return (y_p, y_s.reshape(bs, ts, d),
            kp.reshape(1, bp, tp, 2 * H_DIFF, DH_DIFF), vp.reshape(1, bp, tp, H_DIFF, DV_DIFF), ret_p[None],
            ks.reshape(1, bs, ts, 2 * H_DIFF, DH_DIFF), vs.reshape(1, bs, ts, H_DIFF, DV_DIFF), ret_s[None])
```

```python
import functools
import math

import jax
import jax.numpy as jnp
import numpy as np
from jax import lax
from jax.experimental import pallas as pl
from jax.experimental.pallas import tpu as pltpu

F32 = jnp.float32
BF16 = jnp.bfloat16

V7X_LANES = 128
V7X_VMEM_BYTES = 64 * 1024 * 1024
VMEM_LIMIT = 56 * 1024 * 1024

D_MODEL = 1024
PAGE_SIZE = 128
H_DIFF = 4
DH_DIFF = 64
DV_DIFF = 128
W_DIFF = H_DIFF * DV_DIFF
H_RET = 4
DK_RET = 128
DV_RET = 128
W_RET = H_RET * DV_RET
IN_COLS = 7 * 512
RET_CHUNK = 256
ROPE_THETA = 10000.0
N_GROUPS = 4
EXPERTS_PER_GROUP = 4
N_EXPERTS = 16
D_EXPERT = 256
NORM_EPS = 1e-6
GN_EPS = 1e-5
LAM_INIT = 0.8 - 0.6 * math.exp(-0.3 * 0)
LOG2E = 1.4426950408889634
NEG_BIG = -1e30
LOG_G = tuple(math.log1p(-2.0 ** (-5.0 - h)) for h in range(H_RET))
ROUTER_E0 = N_GROUPS


def _silu(x):
    return x * jax.nn.sigmoid(x)


def _adaln_kernel(c_ref, w_ref, b_ref, o_ref):
    s = _silu(c_ref[...])
    o_ref[...] = jnp.dot(s, w_ref[...], preferred_element_type=F32,
                         precision=lax.Precision.HIGHEST) + b_ref[...]


def _adaln(c, w_ada, b_ada):
    n, d = c.shape
    cols = w_ada.shape[1]
    bn = 1024
    return pl.pallas_call(
        _adaln_kernel,
        grid=(cols // bn,),
        in_specs=[pl.BlockSpec((n, d), lambda j: (0, 0)),
                  pl.BlockSpec((d, bn), lambda j: (0, j)),
                  pl.BlockSpec((1, bn), lambda j: (0, j))],
        out_specs=pl.BlockSpec((n, bn), lambda j: (0, j)),
        out_shape=jax.ShapeDtypeStruct((n, cols), F32),
        name="adaln",
    )(c, w_ada, b_ada.reshape(1, cols))


def _rope64(z, cos, sin_signed, first_half):
    rot = jnp.where(first_half, pltpu.roll(z, 96, 1), pltpu.roll(z, 32, 1))
    return z * cos + rot * sin_signed


def _rope128(z, cos, sin_signed):
    return z * cos + pltpu.roll(z, 64, 1) * sin_signed


def _inproj_kernel(x_ref, g_ref, sc_ref, sh_ref, w_ref, cd_ref, sd_ref, cr_ref, sr_ref,
                   kout_ref, vout_ref, q_ref, kb_ref, v_ref, *rest, transposed, fuse_retention):
    tm = x_ref.shape[1]
    if fuse_retention:
        rf_ref, rout_ref, state = rest
        t = pl.program_id(1)

        @pl.when(t == 0)
        def _():
            state[...] = jnp.zeros(state.shape, F32)
    else:
        qr_ref, kr_ref, vr_ref, gr_ref = rest
    x = x_ref[0]
    y = x * lax.rsqrt(jnp.mean(x * x, axis=-1, keepdims=True) + NORM_EPS)
    h = (y * g_ref[...]) * (1.0 + sc_ref[0]) + sh_ref[0]
    hb = h.astype(BF16)

    def proj(section):
        return jnp.dot(hb, w_ref[:, section * 512:(section + 1) * 512], preferred_element_type=F32)

    zq, zk, zv, zqr, zkr, zvr, zgr = [proj(i) for i in range(7)]

    cd, sd, cr, sr = cd_ref[...], sd_ref[...], cr_ref[...], sr_ref[...]
    lane = lax.broadcasted_iota(jnp.int32, (tm, V7X_LANES), 1)
    first_half = (lane % DH_DIFF) < (DH_DIFF // 2)
    q_scale = (DH_DIFF ** -0.5) * LOG2E
    for j in range(4):
        cols = slice(j * V7X_LANES, (j + 1) * V7X_LANES)
        q = _rope64(zq[:, cols], cd, sd, first_half) * q_scale
        k = _rope64(zk[:, cols], cd, sd, first_half)
        v = zv[:, cols]
        kout_ref[0, :, cols] = k
        vout_ref[0, pl.ds(j, tm, stride=H_DIFF), :] = v
        kb_ref[0, :, cols] = k.astype(BF16)
        if transposed:
            q_ref[0, cols, :] = q.T.astype(BF16)
            v_ref[0, cols, :] = v.T.astype(BF16)
        else:
            q_ref[0, :, cols] = q.astype(BF16)
            v_ref[0, :, cols] = v.astype(BF16)
        qr = _rope128(zqr[:, cols], cr, sr).astype(BF16)
        kr = (_rope128(zkr[:, cols], cr, sr) * (DK_RET ** -0.5)).astype(BF16)
        vr = zvr[:, cols].astype(BF16)
        gr = zgr[:, cols].astype(BF16)
        if fuse_retention:
            operands = (qr, kr, vr, gr)

            def get(i, rows, operands=operands):
                return operands[i][rows]

            def put(rows, val, cols=cols):
                rf_ref[0, rows, cols] = val.astype(rf_ref.dtype)

            for unit in _retention_units(get, state, j, put, chunk=RET_CHUNK, chunk_len=RET_CHUNK,
                                         n_chunks=tm // RET_CHUNK):
                unit()
        else:
            qr_ref[0, :, cols] = qr
            kr_ref[0, :, cols] = kr
            vr_ref[0, :, cols] = vr
            gr_ref[0, :, cols] = gr

    if fuse_retention:
        @pl.when(t == pl.num_programs(1) - 1)
        def _():
            rout_ref[0] = state[...]


def _mod_spec(mod, d, col):
    r = mod.shape[1]
    if r == 1:
        return pl.BlockSpec((1, 1, d), lambda b, t: (b, 0, col))
    return pl.BlockSpec((1, r, d), lambda b, t: (b, t, col))


MOD_SH1, MOD_SC1, MOD_GT1, MOD_SH2, MOD_SC2, MOD_GT2 = range(6)


def _inproj(x, g, mod, w_in_b, tables, *, tm, transposed, fuse_retention):
    bx, tx, d = x.shape
    nt = tx // tm
    tab_spec = pl.BlockSpec((tm, V7X_LANES), lambda b, t: (t, 0))
    nat = lambda dt: jax.ShapeDtypeStruct((bx, tx, 512), dt)
    nat_spec = pl.BlockSpec((1, tm, 512), lambda b, t: (b, t, 0))
    if transposed:
        tr = jax.ShapeDtypeStruct((bx, 512, tx), BF16)
        tr_spec = pl.BlockSpec((1, 512, tm), lambda b, t: (b, 0, t))
    else:
        tr, tr_spec = nat(BF16), nat_spec
    vrows = jax.ShapeDtypeStruct((bx, tx * H_DIFF, DV_DIFF), F32)
    vrows_spec = pl.BlockSpec((1, tm * H_DIFF, DV_DIFF), lambda b, t: (b, t, 0))
    out_shape = [nat(F32), vrows, tr, nat(BF16), tr]
    out_specs = [nat_spec, vrows_spec, tr_spec, nat_spec, tr_spec]
    scratch = []
    if fuse_retention:
        out_shape += [nat(BF16), jax.ShapeDtypeStruct((bx, H_RET, DK_RET, DV_RET), F32)]
        out_specs += [nat_spec, pl.BlockSpec((1, H_RET, DK_RET, DV_RET), lambda b, t: (b, 0, 0, 0))]
        scratch = [pltpu.VMEM((H_RET, DK_RET, DV_RET), F32)]
    else:
        out_shape += [nat(BF16)] * 4
        out_specs += [nat_spec] * 4
    return pl.pallas_call(
        functools.partial(_inproj_kernel, transposed=transposed, fuse_retention=fuse_retention),
        grid=(bx, nt),
        scratch_shapes=scratch,
        in_specs=[pl.BlockSpec((1, tm, d), lambda b, t: (b, t, 0)),
                  pl.BlockSpec((1, d), lambda b, t: (0, 0)),
                  _mod_spec(mod, d, MOD_SC1), _mod_spec(mod, d, MOD_SH1),
                  pl.BlockSpec((d, IN_COLS), lambda b, t: (0, 0)),
                  tab_spec, tab_spec, tab_spec, tab_spec],
        out_specs=out_specs,
        out_shape=out_shape,
        compiler_params=pltpu.CompilerParams(
            dimension_semantics=("arbitrary", "arbitrary"), vmem_limit_bytes=VMEM_LIMIT),
        name="inproj",
    )(x, g, mod, mod, w_in_b, *tables)


def _rope_tables(pos):
    lane = np.arange(V7X_LANES)
    pos = np.asarray(pos, np.float64)

    def one(d):
        half = d // 2
        inv = np.power(ROPE_THETA, -(2.0 / d) * np.arange(half, dtype=np.float64))
        ang = pos[:, None] * inv[None, :]
        idx = lane % half
        sign = np.where((lane % d) < half, -1.0, 1.0)
        return (jnp.asarray(np.cos(ang)[:, idx], F32), jnp.asarray(np.sin(ang)[:, idx] * sign[None, :], F32))

    cd, sd = one(DH_DIFF)
    cr, sr = one(DK_RET)
    return cd, sd, cr, sr


def _diff_lambda(lq1_ref, lk1_ref, lq2_ref, lk2_ref):
    e1 = jnp.exp(jnp.sum(lq1_ref[...] * lk1_ref[...], axis=-1, keepdims=True))
    e2 = jnp.exp(jnp.sum(lq2_ref[...] * lk2_ref[...], axis=-1, keepdims=True))
    return e1 - e2 + LAM_INIT


ATTN_SW = 256
ATTN_KB = 256
ATTN_TQ = 2 * ATTN_KB
ATTN_ONES_ROWS = 16


def _attn_kernel(qT_ref, k_ref, vT_ref, lq1_ref, lk1_ref, lq2_ref, lk2_ref, o_ref,
                 m_sc, l_sc, acc_sc, s_buf, p_buf, a_buf, *, tq):
    sw, kb = ATTN_SW, ATTN_KB
    assert tq == 2 * kb and tq % sw == 0
    ns = tq // sw
    nh = qT_ref.shape[1] // V7X_LANES
    qi = pl.program_id(2)
    row = lax.broadcasted_iota(jnp.int32, (V7X_LANES, tq), 0)
    zero = jnp.zeros((V7X_LANES, tq), BF16)
    q_sub = []
    for hh in range(nh):
        qT = qT_ref[0, hh * V7X_LANES:(hh + 1) * V7X_LANES, :]
        q_sub.append((jnp.where(row < DH_DIFF, qT, zero), jnp.where(row >= DH_DIFF, qT, zero)))
    m_sc[...] = jnp.full(m_sc.shape, NEG_BIG, F32)
    l_sc[...] = jnp.zeros(l_sc.shape, F32)
    acc_sc[...] = jnp.zeros(acc_sc.shape, F32)
    chains = [(hh, a, st) for hh in range(nh) for a in range(2) for st in range(ns)]
    chain_id = lambda hh, a, st: (hh * 2 + a) * ns + st
    head_lanes = lambda hh: slice(hh * V7X_LANES, (hh + 1) * V7X_LANES)

    def scores(slot, key_base, diag_step):
        key0 = pl.multiple_of(key_base, kb)
        for hh, a, st in chains:
            off = 0 if diag_step is None else diag_step * kb - st * sw
            if off > sw - 1:
                s_buf[slot, chain_id(hh, a, st)] = jnp.full((kb, sw), -jnp.inf, F32)
                continue
            kblk = k_ref[0, pl.ds(key0, kb), head_lanes(hh)]
            s = jnp.dot(kblk, q_sub[hh][a][:, st * sw:(st + 1) * sw], preferred_element_type=F32)
            if diag_step is not None:
                if off + kb - 1 > 0:
                    kpos = off + lax.broadcasted_iota(jnp.int32, (kb, sw), 0)
                    qpos = lax.broadcasted_iota(jnp.int32, (kb, sw), 1)
                    s = jnp.where(kpos <= qpos, s, -jnp.inf)
            s_buf[slot, chain_id(hh, a, st)] = s

    def softmax(slot):
        for hh, a, st in chains:
            c = chain_id(hh, a, st)
            m_old = m_sc[c]
            m_new = jnp.maximum(m_old, jnp.max(s_buf[slot, c], axis=0, keepdims=True))
            alpha = jnp.exp2(m_old - m_new)
            p = jnp.exp2(s_buf[slot, c] - m_new)
            m_sc[c] = m_new
            a_buf[slot, c] = alpha
            p_buf[slot, c] = p.astype(BF16)

    def values(slot, key_base):
        key0 = pl.multiple_of(key_base, kb)
        ones = jnp.ones((ATTN_ONES_ROWS, kb), BF16)
        for hh, a, st in chains:
            c = chain_id(hh, a, st)
            vblk = jnp.concatenate([vT_ref[0, head_lanes(hh), pl.ds(key0, kb)], ones], axis=0)
            pv = jnp.dot(vblk, p_buf[slot, c], preferred_element_type=F32)
            alpha = a_buf[slot, c]
            acc_sc[c] = acc_sc[c] * alpha + pv[:DV_DIFF]
            l_sc[c] = l_sc[c] * alpha + pv[DV_DIFF:DV_DIFF + 1]

    diag_base = [qi * tq + d * kb for d in range(2)]
    scores(0, diag_base[0], 0)
    scores(1, diag_base[1], 1)
    softmax(0)

    def trip_pair(u, carry):
        for par_t in range(2):
            full_base = (2 * u + par_t) * kb
            lag_base = jnp.where(u == 0, diag_base[par_t], full_base - 2 * kb)
            scores(par_t, full_base, None)
            softmax(1 - par_t)
            values(par_t, lag_base)
        return carry

    lax.fori_loop(0, qi, trip_pair, 0)
    last = [jnp.where(qi == 0, diag_base[d], (2 * qi - 2 + d) * kb) for d in range(2)]
    softmax(1)
    values(0, last[0])
    values(1, last[1])

    lam = _diff_lambda(lq1_ref, lk1_ref, lq2_ref, lk2_ref)
    for hh in range(nh):
        for st in range(ns):
            outs = [acc_sc[chain_id(hh, a, st)] / l_sc[chain_id(hh, a, st)] for a in range(2)]
            o = outs[0] - lam * outs[1]
            of = o * lax.rsqrt(jnp.mean(o * o, axis=0, keepdims=True) + NORM_EPS) * (1.0 - LAM_INIT)
            o_ref[0, st * sw:(st + 1) * sw, head_lanes(hh)] = of.T.astype(o_ref.dtype)


def _attn_prompt(qT, kb, vT, lams, *, tq, heads):
    b, _, t = qT.shape
    lam_spec = pl.BlockSpec((1, DH_DIFF), lambda b_, h, i: (0, 0))
    n_chains = heads * 2 * (tq // ATTN_SW)
    hw = heads * V7X_LANES
    return pl.pallas_call(
        functools.partial(_attn_kernel, tq=tq),
        grid=(b, H_DIFF // heads, t // tq),
        in_specs=[pl.BlockSpec((1, hw, tq), lambda b_, h, i: (b_, h, i)),
                  pl.BlockSpec((1, t, hw), lambda b_, h, i: (b_, 0, h)),
                  pl.BlockSpec((1, hw, t), lambda b_, h, i: (b_, h, 0)),
                  lam_spec, lam_spec, lam_spec, lam_spec],
        out_specs=pl.BlockSpec((1, tq, hw), lambda b_, h, i: (b_, i, h)),
        out_shape=jax.ShapeDtypeStruct((b, t, W_DIFF), BF16),
        scratch_shapes=[pltpu.VMEM((n_chains, 1, ATTN_SW), F32), pltpu.VMEM((n_chains, 1, ATTN_SW), F32),
                        pltpu.VMEM((n_chains, DV_DIFF, ATTN_SW), F32),
                        pltpu.VMEM((2, n_chains, ATTN_KB, ATTN_SW), F32),
                        pltpu.VMEM((2, n_chains, ATTN_KB, ATTN_SW), BF16),
                        pltpu.VMEM((2, n_chains, 1, ATTN_SW), F32)],
        compiler_params=pltpu.CompilerParams(
            dimension_semantics=("arbitrary", "arbitrary", "arbitrary"), vmem_limit_bytes=VMEM_LIMIT),
        name="attn_prompt",
    )(qT, kb, vT, *lams)


def _decode_kernel(pt_ref, q_ref, kn_ref, vn_ref, lq1_ref, lk1_ref, lq2_ref, lk2_ref, *rest, pages, ts):
    k_refs, v_refs = rest[:pages], rest[pages:2 * pages]
    o_ref, m_sc, l_sc, acc_sc = rest[2 * pages:]
    s_id = pl.program_id(1)
    q = q_ref[0]
    hr = 2 * ts

    @pl.when(s_id == 0)
    def _():
        m_sc[...] = jnp.full(m_sc.shape, NEG_BIG, F32)
        l_sc[...] = jnp.zeros(l_sc.shape, F32)
        acc_sc[...] = jnp.zeros(acc_sc.shape, F32)

    def update(s, v_heads):
        m_old = m_sc[:, 0:1]
        m_new = jnp.maximum(m_old, jnp.max(s, axis=1, keepdims=True))
        alpha = jnp.exp2(m_old - m_new)
        p = jnp.exp2(s - m_new)
        l_new = alpha * l_sc[:, 0:1] + jnp.sum(p, axis=1, keepdims=True)
        pb = p.astype(BF16)
        for h in range(H_DIFF):
            rows = slice(h * hr, (h + 1) * hr)
            pv = jnp.dot(pb[rows, :], v_heads[h], preferred_element_type=F32)
            acc_sc[rows, :] = acc_sc[rows, :] * alpha[rows, :] + pv
        m_sc[...] = jnp.broadcast_to(m_new, m_sc.shape)
        l_sc[...] = jnp.broadcast_to(l_new, l_sc.shape)

    kT = jnp.concatenate([kr[0].astype(BF16) for kr in k_refs], axis=1)
    s_past = jnp.dot(q, kT, preferred_element_type=F32)
    update(s_past, [jnp.concatenate([vr[0, pl.ds(h, PAGE_SIZE, stride=H_DIFF), :].astype(BF16)
                                     for vr in v_refs], axis=0) for h in range(H_DIFF)])

    @pl.when(s_id == pl.num_programs(1) - 1)
    def _():
        s_new = lax.dot_general(q, kn_ref[0], (((1,), (1,)), ((), ())), preferred_element_type=F32)
        key = lax.broadcasted_iota(jnp.int32, s_new.shape, 1)
        tok = lax.broadcasted_iota(jnp.int32, s_new.shape, 0) % ts
        s_new = jnp.where(key <= tok, s_new, NEG_BIG)
        update(s_new, [vn_ref[0, :, h * DV_DIFF:(h + 1) * DV_DIFF] for h in range(H_DIFF)])
        lam = _diff_lambda(lq1_ref, lk1_ref, lq2_ref, lk2_ref)
        on = acc_sc[...] / l_sc[:, 0:1]
        for h in range(H_DIFF):
            o = on[h * hr:h * hr + ts, :] - lam * on[h * hr + ts:(h + 1) * hr, :]
            of = o * lax.rsqrt(jnp.mean(o * o, axis=-1, keepdims=True) + NORM_EPS) * (1.0 - LAM_INIT)
            o_ref[0, :, h * DV_DIFF:(h + 1) * DV_DIFF] = of


def _attn_sample(q_rows, k_new, v_new, cache_k, cache_v, page_table, lams, *, pages, ts):
    bs, n_pages = page_table.shape
    rows = q_rows.shape[1]
    steps = n_pages // pages
    lam_spec = pl.BlockSpec((1, DH_DIFF), lambda b, s, pt: (0, 0))

    assert 2 * ts == 8, "one head's query rows must fill one sublane tile"

    def page_spec(i):
        return pl.BlockSpec((1, 512, V7X_LANES), lambda b, s, pt: (pt[b, s * pages + i], 0, 0))

    per_b = lambda r: pl.BlockSpec((1, r, 512), lambda b, s, pt: (b, 0, 0))
    grid_spec = pltpu.PrefetchScalarGridSpec(
        num_scalar_prefetch=1,
        grid=(bs, steps),
        in_specs=[per_b(rows), per_b(PAGE_SIZE), per_b(PAGE_SIZE), lam_spec, lam_spec, lam_spec, lam_spec]
                 + [page_spec(i) for i in range(pages)] + [page_spec(i) for i in range(pages)],
        out_specs=per_b(ts),
        scratch_shapes=[pltpu.VMEM((rows, V7X_LANES), F32), pltpu.VMEM((rows, V7X_LANES), F32),
                        pltpu.VMEM((rows, DV_DIFF), F32)],
    )
    return pl.pallas_call(
        functools.partial(_decode_kernel, pages=pages, ts=ts),
        grid_spec=grid_spec,
        out_shape=jax.ShapeDtypeStruct((bs, ts, W_DIFF), F32),
        compiler_params=pltpu.CompilerParams(
            dimension_semantics=("arbitrary", "arbitrary"), vmem_limit_bytes=VMEM_LIMIT),
        name="attn_sample",
    )(page_table, q_rows, k_new, v_new, *lams, *([cache_k] * pages), *([cache_v] * pages))


def _retention_units(get, state, h, put, *, chunk, chunk_len, n_chunks):
    ii = lax.broadcasted_iota(jnp.int32, (chunk, chunk), 0)
    jj = lax.broadcasted_iota(jnp.int32, (chunk, chunk), 1)
    dif = (ii - jj).astype(F32)
    tok = lax.broadcasted_iota(jnp.int32, (chunk, DK_RET), 0).astype(F32)
    decay = jnp.where(dif >= 0, jnp.exp(jnp.maximum(dif, 0.0) * LOG_G[h]), 0.0)
    q_dec = jnp.exp((tok + 1.0) * LOG_G[h])
    k_dec = jnp.exp((chunk_len - 1.0 - tok) * LOG_G[h])
    chunk_dec = math.exp(chunk_len * LOG_G[h])

    def unit(c):
        rows = slice(c * chunk, (c + 1) * chunk)
        qc, kc, vc = get(0, rows), get(1, rows), get(2, rows)
        r_prev = state[h]
        s = lax.dot_general(qc, kc, (((1,), (1,)), ((), ())), preferred_element_type=F32) * decay
        o = (jnp.dot(s.astype(qc.dtype), vc, preferred_element_type=F32)
             + jnp.dot(qc, r_prev.astype(qc.dtype), preferred_element_type=F32) * q_dec)
        vd = (vc.astype(F32) * k_dec).astype(qc.dtype)
        kv = lax.dot_general(kc, vd, (((0,), (0,)), ((), ())), preferred_element_type=F32)
        state[h] = r_prev * chunk_dec + kv
        mu = jnp.mean(o, axis=-1, keepdims=True)
        var = jnp.mean(jnp.square(o - mu), axis=-1, keepdims=True)
        nrm = (o - mu) * lax.rsqrt(var + GN_EPS)
        put(rows, nrm * _silu(get(3, rows).astype(F32)))

    return [functools.partial(unit, c) for c in range(n_chunks)]


def _retention_kernel(q_ref, k_ref, v_ref, g_ref, r0_ref, o_ref, rout_ref, state, *, chunk, chunk_len, n_chunks):
    t = pl.program_id(1)
    refs = (q_ref, k_ref, v_ref, g_ref)

    @pl.when(t == 0)
    def _():
        state[...] = r0_ref[...]

    for bi in range(q_ref.shape[0]):
        for h in range(H_RET):
            cols = slice(h * DK_RET, (h + 1) * DK_RET)

            def get(i, rows, bi=bi, cols=cols):
                return refs[i][bi, rows, cols]

            def put(rows, val, bi=bi, cols=cols):
                o_ref[bi, rows, cols] = val.astype(o_ref.dtype)

            for unit in _retention_units(get, state.at[bi], h, put, chunk=chunk, chunk_len=chunk_len,
                                         n_chunks=n_chunks):
                unit()

    @pl.when(t == pl.num_programs(1) - 1)
    def _():
        rout_ref[...] = state[...]


def _retention(qr, kr, vr, gr, r0, *, chunk, chunk_len, n_chunks, seqs):
    b, t, _ = qr.shape
    tr = chunk * n_chunks
    tok_spec = pl.BlockSpec((seqs, tr, 512), lambda b_, i: (b_, i, 0))
    st_spec = pl.BlockSpec((seqs, H_RET, DK_RET, DV_RET), lambda b_, i: (b_, 0, 0, 0))
    return pl.pallas_call(
        functools.partial(_retention_kernel, chunk=chunk, chunk_len=chunk_len, n_chunks=n_chunks),
        grid=(b // seqs, t // tr),
        in_specs=[tok_spec] * 4 + [st_spec],
        out_specs=[tok_spec, st_spec],
        out_shape=[jax.ShapeDtypeStruct((b, t, W_RET), BF16),
                   jax.ShapeDtypeStruct((b, H_RET, DK_RET, DV_RET), F32)],
        scratch_shapes=[pltpu.VMEM((seqs, H_RET, DK_RET, DV_RET), F32)],
        compiler_params=pltpu.CompilerParams(
            dimension_semantics=("arbitrary", "arbitrary"), vmem_limit_bytes=VMEM_LIMIT),
        name="retention",
    )(qr, kr, vr, gr, r0)


def _route(z):
    lane = lax.broadcasted_iota(jnp.int32, z.shape, 1).astype(F32)
    big = 1e9
    neg = -jnp.inf
    is_g = lane < N_GROUPS
    glog = jnp.where(is_g, z, neg)
    gmax = jnp.max(glog, axis=-1, keepdims=True)
    gsel = jnp.min(jnp.where(glog == gmax, lane, big), axis=-1, keepdims=True)
    psum = jnp.sum(jnp.where(is_g, jnp.exp(z - gmax), 0.0), axis=-1, keepdims=True)
    pg_sel = 1.0 / psum
    rel = jnp.zeros_like(z)
    for g in range(N_GROUPS):
        first = ROUTER_E0 + g * EXPERTS_PER_GROUP
        rel = jnp.where(gsel == float(g), pltpu.roll(z, V7X_LANES - first, 1), rel)
    elog = jnp.where(lane < EXPERTS_PER_GROUP, rel, neg)
    v1 = jnp.max(elog, axis=-1, keepdims=True)
    i1 = jnp.min(jnp.where(elog == v1, lane, big), axis=-1, keepdims=True)
    elog2 = jnp.where(lane == i1, neg, elog)
    v2 = jnp.max(elog2, axis=-1, keepdims=True)
    i2 = jnp.min(jnp.where(elog2 == v2, lane, big), axis=-1, keepdims=True)
    e2 = jnp.exp(v2 - v1)
    den = 1.0 + e2
    w1 = (1.0 / den) * pg_sel
    w2 = (e2 / den) * pg_sel
    return gsel, jnp.where(lane == i1, w1, jnp.where(lane == i2, w2, 0.0))


MOE_CHUNK = 160
MOE_PAD = 256


def _ffn_kernel(x_ref, od_ref, rf_ref, beta_ref, wout_ref, gt1_ref, gffn_ref, sc2_ref, sh2_ref, gt2_ref,
                wr_ref, br_ref, tri_ref, wgu_ref, wd_ref, gfin_ref, y_ref,
                h2_sc, g_sc, pos_sc, posr_sc, moe_sc, *, chunk):
    tm = x_ref.shape[1]
    beta = beta_ref[...]
    yd = (od_ref[0].astype(F32) * beta[:, :W_DIFF]).astype(BF16)
    yr = (rf_ref[0].astype(F32) * beta[:, W_DIFF:]).astype(BF16)
    mix = (jnp.dot(yd, wout_ref[:W_DIFF, :], preferred_element_type=F32)
           + jnp.dot(yr, wout_ref[W_DIFF:, :], preferred_element_type=F32))
    x1 = x_ref[0] + gt1_ref[0] * mix
    n2 = x1 * lax.rsqrt(jnp.mean(x1 * x1, axis=-1, keepdims=True) + NORM_EPS)
    h2 = (n2 * gffn_ref[...]) * (1.0 + sc2_ref[0]) + sh2_ref[0]
    h_hi = h2.astype(BF16)
    h_lo = (h2 - h_hi.astype(F32)).astype(BF16)
    zz = jnp.dot(h_hi, wr_ref[...], preferred_element_type=F32)
    z = (zz[:, :V7X_LANES] + zz[:, V7X_LANES:]
         + jnp.dot(h_lo, wr_ref[:, :V7X_LANES], preferred_element_type=F32) + br_ref[...])
    gsel, gates = _route(z)
    h2_sc[...] = h_hi
    g_hi = gates.astype(BF16)
    g_sc[0] = g_hi
    g_sc[1] = (gates - g_hi.astype(F32)).astype(BF16)

    lane = lax.broadcasted_iota(jnp.int32, (tm, V7X_LANES), 1).astype(F32)
    lane1 = lax.broadcasted_iota(jnp.int32, (1, V7X_LANES), 1)
    onehot = jnp.where(lane == gsel, 1.0, 0.0)
    earlier = jnp.dot(tri_ref[...], onehot.astype(BF16), preferred_element_type=F32)
    rank = jnp.sum(onehot * earlier, axis=-1, keepdims=True)
    cnt = jnp.sum(onehot, axis=0, keepdims=True)
    n_chunks = jnp.zeros_like(cnt)
    for m in range(-(-tm // chunk)):
        n_chunks = n_chunks + jnp.where(cnt > m * chunk, 1.0, 0.0)
    first_chunk = [jnp.int32(0)]
    for g in range(N_GROUPS):
        first_chunk.append(first_chunk[-1] + n_chunks[0, g].astype(jnp.int32))
    first_vec = jnp.zeros((1, V7X_LANES), F32)
    for g in range(N_GROUPS):
        first_vec = jnp.where(lane1 == g, first_chunk[g].astype(F32), first_vec)
    pos = jnp.sum(onehot * first_vec, axis=-1, keepdims=True) * chunk + rank
    pos_sc[...] = jnp.broadcast_to(pos, pos_sc.shape)
    posr_sc[...] = jnp.broadcast_to(pos, (tm, V7X_LANES)).T[0:8, :]
    moe_sc[...] = jnp.zeros(moe_sc.shape, F32)

    col_id = lax.broadcasted_iota(jnp.int32, (tm, MOE_PAD), 1)
    col_f = col_id.astype(F32)
    row_f = lax.broadcasted_iota(jnp.int32, (chunk, tm), 0).astype(F32)

    def chunk_body(k, carry):
        g = (jnp.where(k >= first_chunk[1], 1, 0) + jnp.where(k >= first_chunk[2], 1, 0)
             + jnp.where(k >= first_chunk[3], 1, 0))
        off = (k * chunk).astype(F32)
        gather = jnp.where(posr_sc[0:1, :] - off == row_f, 1.0, 0.0).astype(BF16)
        scatter = jnp.where((pos_sc[...] - off == col_f) & (col_id < chunk), 1.0, 0.0).astype(BF16)
        hc = jnp.dot(gather, h2_sc[...], preferred_element_type=F32).astype(BF16)
        gc = (jnp.dot(gather, g_sc[0], preferred_element_type=F32)
              + jnp.dot(gather, g_sc[1], preferred_element_type=F32))
        hh = []
        for e4 in range(EXPERTS_PER_GROUP):
            au = jnp.dot(hc, wgu_ref[g * EXPERTS_PER_GROUP + e4], preferred_element_type=F32)
            a, u = au[:, :D_EXPERT], au[:, D_EXPERT:]
            hh.append((_silu(a) * u * gc[:, e4:e4 + 1]).astype(BF16))
        oc = jnp.dot(jnp.concatenate(hh, axis=1), wd_ref[g], preferred_element_type=F32)
        oc = jnp.concatenate([oc.astype(BF16), jnp.zeros((MOE_PAD - chunk, oc.shape[1]), BF16)], axis=0)
        moe_sc[...] += jnp.dot(scatter, oc, preferred_element_type=F32)
        return carry

    lax.fori_loop(0, first_chunk[N_GROUPS], chunk_body, 0)
    x2 = x1 + gt2_ref[0] * moe_sc[...]
    y_ref[0] = (x2 * lax.rsqrt(jnp.mean(x2 * x2, axis=-1, keepdims=True) + NORM_EPS)) * gfin_ref[...]


def _ffn(x, od, rf, mod, weights, *, tm):
    bx, tx, d = x.shape
    beta, w_out_b, g_ffn, w_router, b_router, wgu_b, wd_b, g_fin = weights
    chunk = min(MOE_CHUNK, tm)
    tri = jnp.asarray(np.tril(np.ones((tm, tm), np.float32), -1), BF16)
    const = lambda shape: pl.BlockSpec(shape, lambda b, t: (0,) * len(shape), pipeline_mode=pl.Buffered(1))
    tok = lambda w: pl.BlockSpec((1, tm, w), lambda b, t: (b, t, 0))
    gw = EXPERTS_PER_GROUP * D_EXPERT
    return pl.pallas_call(
        functools.partial(_ffn_kernel, chunk=chunk),
        grid=(bx, tx // tm),
        in_specs=[tok(d), tok(W_DIFF), tok(W_RET), const((1, d)), const((d, d)), _mod_spec(mod, d, MOD_GT1),
                  const((1, d)), _mod_spec(mod, d, MOD_SC2), _mod_spec(mod, d, MOD_SH2), _mod_spec(mod, d, MOD_GT2),
                  const((d, 2 * V7X_LANES)), const((1, V7X_LANES)),
                  const((tm, tm)), const((N_EXPERTS, d, 2 * D_EXPERT)), const((N_GROUPS, gw, d)), const((1, d))],
        out_specs=tok(d),
        out_shape=jax.ShapeDtypeStruct((bx, tx, d), F32),
        scratch_shapes=[pltpu.VMEM((tm, d), BF16), pltpu.VMEM((2, tm, V7X_LANES), BF16),
                        pltpu.VMEM((tm, MOE_PAD), F32), pltpu.VMEM((8, tm), F32),
                        pltpu.VMEM((tm, d), F32)],
        compiler_params=pltpu.CompilerParams(
            dimension_semantics=("arbitrary", "arbitrary"), vmem_limit_bytes=VMEM_LIMIT),
        name="ffn",
    )(x, od, rf, beta, w_out_b, mod, g_ffn, mod, mod, mod, w_router, b_router, tri, wgu_b, wd_b, g_fin)


def kernel(x_prompt, x_sample, cache_k, cache_v, state_ret, page_table, c_prompt, c_sample, w_ada, b_ada,
           norm_mix_g, norm_ffn_g, w_in, lambda_q1, lambda_k1, lambda_q2, lambda_k2, beta_mix, w_out,
           w_group, b_group, w_expert_router, b_expert_router, w_gate_e, w_up_e, w_down_e, final_g):
    assert w_ada.shape[0] == 1, "single-layer stack only"
    bp, tp, d = x_prompt.shape
    bs, ts, _ = x_sample.shape
    n_pages = page_table.shape[1]
    past_len = n_pages * PAGE_SIZE
    n_phys = cache_k.shape[1]

    w_in_b = w_in[0].astype(BF16)
    w_out_b = w_out[0].astype(BF16)
    wgu_b = jnp.concatenate([w_gate_e[0], w_up_e[0]], axis=-1).astype(BF16)
    wd_b = w_down_e[0].reshape(N_GROUPS, EXPERTS_PER_GROUP * D_EXPERT, d).astype(BF16)
    w_er = jnp.transpose(w_expert_router[0], (1, 0, 2)).reshape(d, N_EXPERTS)
    pad = V7X_LANES - N_GROUPS - N_EXPERTS
    w_router = jnp.concatenate([w_group[0], w_er, jnp.zeros((d, pad), F32)], axis=1)
    w_router_hi = w_router.astype(BF16)
    w_router = jnp.concatenate([w_router_hi, (w_router - w_router_hi.astype(F32)).astype(BF16)], axis=1)
    b_router = jnp.concatenate([b_group[0], b_expert_router[0].reshape(N_EXPERTS), jnp.zeros((pad,), F32)])[None]
    ffn_weights = (beta_mix, w_out_b, norm_ffn_g, w_router, b_router, wgu_b, wd_b, final_g[None])
    lams = (lambda_q1, lambda_k1, lambda_q2, lambda_k2)

    mod = _adaln(jnp.concatenate([c_prompt, c_sample], axis=0), w_ada[0], b_ada[0])
    mod_p = mod[:bp, None, :]
    mod_s = jnp.repeat(mod[bp:], ts, axis=0)[None]

    tabs_p = _rope_tables(np.arange(tp))
    kp, vp, qT, kb, vT, rf_p, ret_p = _inproj(
        x_prompt, norm_mix_g, mod_p, w_in_b, tabs_p, tm=512, transposed=True, fuse_retention=True)
    od_p = _attn_prompt(qT, kb, vT, lams, tq=ATTN_TQ, heads=2)
    y_p = _ffn(x_prompt, od_p, rf_p, mod_p, ffn_weights, tm=512)

    rows_s = bs * ts
    tabs_s = _rope_tables(np.tile(past_len + np.arange(ts), bs))
    ks, vs, q_s, kb_s, vb_s, qr_s, kr_s, vr_s, gr_s = _inproj(
        x_sample.reshape(1, rows_s, d), norm_mix_g, mod_s, w_in_b, tabs_s, tm=rows_s, transposed=False,
        fuse_retention=False)
    sub = jnp.arange(2 * H_DIFF)
    col_sub = jnp.arange(512) // DH_DIFF
    q_rows = jnp.where((sub[:, None, None] == col_sub[None, None, :])[None],
                       q_s.reshape(bs, 1, ts, 512), jnp.zeros((), BF16)).reshape(bs, 2 * H_DIFF * ts, 512)
    pad_keys = lambda a: jnp.pad(a.reshape(bs, ts, 512), ((0, 0), (0, PAGE_SIZE - ts), (0, 0)))
    ck = jnp.transpose(cache_k[0], (0, 2, 3, 1)).reshape(n_phys, 2 * H_DIFF * DH_DIFF, PAGE_SIZE)
    cv = cache_v[0].reshape(n_phys, PAGE_SIZE * H_DIFF, DV_DIFF)
    od_s = _attn_sample(q_rows, pad_keys(kb_s), pad_keys(vb_s), ck, cv, page_table, lams, pages=32, ts=ts)
    chunk_s = 16
    pad_tok = lambda a: jnp.pad(a.reshape(bs, ts, 512), ((0, 0), (0, chunk_s - ts), (0, 0)))
    rf_s, ret_s = _retention(pad_tok(qr_s), pad_tok(kr_s), pad_tok(vr_s), pad_tok(gr_s), state_ret[0],
                             chunk=chunk_s, chunk_len=ts, n_chunks=1, seqs=8)
    y_s = _ffn(x_sample.reshape(1, rows_s, d), od_s.reshape(1, rows_s, 512),
               rf_s[:, :ts].reshape(1, rows_s, 512), mod_s, ffn_weights, tm=rows_s)

    return (y_p, y_s.reshape(bs, ts, d),
            kp.reshape(1, bp, tp, 2 * H_DIFF, DH_DIFF), vp.reshape(1, bp, tp, H_DIFF, DV_DIFF), ret_p[None],
            ks.reshape(1, bs, ts, 2 * H_DIFF, DH_DIFF), vs.reshape(1, bs, ts, H_DIFF, DV_DIFF), ret_s[None])
```

```python
import functools
import math

import jax
import jax.numpy as jnp
import numpy as np
from jax import lax
from jax.experimental import pallas as pl
from jax.experimental.pallas import tpu as pltpu

F32 = jnp.float32
BF16 = jnp.bfloat16

V7X_LANES = 128
V7X_VMEM_BYTES = 64 * 1024 * 1024
VMEM_LIMIT = 56 * 1024 * 1024

D_MODEL = 1024
PAGE_SIZE = 128
H_DIFF = 4
DH_DIFF = 64
DV_DIFF = 128
W_DIFF = H_DIFF * DV_DIFF
H_RET = 4
DK_RET = 128
DV_RET = 128
W_RET = H_RET * DV_RET
IN_COLS = 7 * 512
RET_CHUNK = 256
ROPE_THETA = 10000.0
N_GROUPS = 4
EXPERTS_PER_GROUP = 4
N_EXPERTS = 16
D_EXPERT = 256
NORM_EPS = 1e-6
GN_EPS = 1e-5
LAM_INIT = 0.8 - 0.6 * math.exp(-0.3 * 0)
LOG2E = 1.4426950408889634
NEG_BIG = -1e30
LOG_G = tuple(math.log1p(-2.0 ** (-5.0 - h)) for h in range(H_RET))
ROUTER_E0 = N_GROUPS


def _silu(x):
    return x * jax.nn.sigmoid(x)


def _adaln_kernel(c_ref, w_ref, b_ref, o_ref):
    s = _silu(c_ref[...])
    o_ref[...] = jnp.dot(s, w_ref[...], preferred_element_type=F32,
                         precision=lax.Precision.HIGHEST) + b_ref[...]


def _adaln(c, w_ada, b_ada):
    n, d = c.shape
    cols = w_ada.shape[1]
    bn = 1024
    return pl.pallas_call(
        _adaln_kernel,
        grid=(cols // bn,),
        in_specs=[pl.BlockSpec((n, d), lambda j: (0, 0)),
                  pl.BlockSpec((d, bn), lambda j: (0, j)),
                  pl.BlockSpec((1, bn), lambda j: (0, j))],
        out_specs=pl.BlockSpec((n, bn), lambda j: (0, j)),
        out_shape=jax.ShapeDtypeStruct((n, cols), F32),
        name="adaln",
    )(c, w_ada, b_ada.reshape(1, cols))


def _rope64(z, cos, sin_signed, first_half):
    rot = jnp.where(first_half, pltpu.roll(z, 96, 1), pltpu.roll(z, 32, 1))
    return z * cos + rot * sin_signed


def _rope128(z, cos, sin_signed):
    return z * cos + pltpu.roll(z, 64, 1) * sin_signed


def _inproj_kernel(x_ref, g_ref, sc_ref, sh_ref, w_ref, cd_ref, sd_ref, cr_ref, sr_ref,
                   kout_ref, vout_ref, q_ref, kb_ref, v_ref, *rest, transposed, fuse_retention):
    tm = x_ref.shape[1]
    if fuse_retention:
        rf_ref, rout_ref, state = rest
        t = pl.program_id(1)

        @pl.when(t == 0)
        def _():
            state[...] = jnp.zeros(state.shape, F32)
    else:
        qr_ref, kr_ref, vr_ref, gr_ref = rest
    x = x_ref[0]
    y = x * lax.rsqrt(jnp.mean(x * x, axis=-1, keepdims=True) + NORM_EPS)
    h = (y * g_ref[...]) * (1.0 + sc_ref[0]) + sh_ref[0]
    hb = h.astype(BF16)

    def proj(section):
        return jnp.dot(hb, w_ref[:, section * 512:(section + 1) * 512], preferred_element_type=F32)

    zq, zk, zv, zqr, zkr, zvr, zgr = [proj(i) for i in range(7)]

    cd, sd, cr, sr = cd_ref[...], sd_ref[...], cr_ref[...], sr_ref[...]
    lane = lax.broadcasted_iota(jnp.int32, (tm, V7X_LANES), 1)
    first_half = (lane % DH_DIFF) < (DH_DIFF // 2)
    q_scale = (DH_DIFF ** -0.5) * LOG2E
    for j in range(4):
        cols = slice(j * V7X_LANES, (j + 1) * V7X_LANES)
        q = _rope64(zq[:, cols], cd, sd, first_half) * q_scale
        k = _rope64(zk[:, cols], cd, sd, first_half)
        v = zv[:, cols]
        kout_ref[0, :, cols] = k
        vout_ref[0, pl.ds(j, tm, stride=H_DIFF), :] = v
        kb_ref[0, :, cols] = k.astype(BF16)
        if transposed:
            q_ref[0, cols, :] = q.T.astype(BF16)
            v_ref[0, cols, :] = v.T.astype(BF16)
        else:
            q_ref[0, :, cols] = q.astype(BF16)
            v_ref[0, :, cols] = v.astype(BF16)
        qr = _rope128(zqr[:, cols], cr, sr).astype(BF16)
        kr = (_rope128(zkr[:, cols], cr, sr) * (DK_RET ** -0.5)).astype(BF16)
        vr = zvr[:, cols].astype(BF16)
        gr = zgr[:, cols].astype(BF16)
        if fuse_retention:
            operands = (qr, kr, vr, gr)

            def get(i, rows, operands=operands):
                return operands[i][rows]

            def put(rows, val, cols=cols):
                rf_ref[0, rows, cols] = val.astype(rf_ref.dtype)

            for unit in _retention_units(get, state, j, put, chunk=RET_CHUNK, chunk_len=RET_CHUNK,
                                         n_chunks=tm // RET_CHUNK):
                unit()
        else:
            qr_ref[0, :, cols] = qr
            kr_ref[0, :, cols] = kr
            vr_ref[0, :, cols] = vr
            gr_ref[0, :, cols] = gr

    if fuse_retention:
        @pl.when(t == pl.num_programs(1) - 1)
        def _():
            rout_ref[0] = state[...]


def _mod_spec(mod, d, col):
    r = mod.shape[1]
    if r == 1:
        return pl.BlockSpec((1, 1, d), lambda b, t: (b, 0, col))
    return pl.BlockSpec((1, r, d), lambda b, t: (b, t, col))


MOD_SH1, MOD_SC1, MOD_GT1, MOD_SH2, MOD_SC2, MOD_GT2 = range(6)


def _inproj(x, g, mod, w_in_b, tables, *, tm, transposed, fuse_retention):
    bx, tx, d = x.shape
    nt = tx // tm
    tab_spec = pl.BlockSpec((tm, V7X_LANES), lambda b, t: (t, 0))
    nat = lambda dt: jax.ShapeDtypeStruct((bx, tx, 512), dt)
    nat_spec = pl.BlockSpec((1, tm, 512), lambda b, t: (b, t, 0))
    if transposed:
        tr = jax.ShapeDtypeStruct((bx, 512, tx), BF16)
        tr_spec = pl.BlockSpec((1, 512, tm), lambda b, t: (b, 0, t))
    else:
        tr, tr_spec = nat(BF16), nat_spec
    vrows = jax.ShapeDtypeStruct((bx, tx * H_DIFF, DV_DIFF), F32)
    vrows_spec = pl.BlockSpec((1, tm * H_DIFF, DV_DIFF), lambda b, t: (b, t, 0))
    out_shape = [nat(F32), vrows, tr, nat(BF16), tr]
    out_specs = [nat_spec, vrows_spec, tr_spec, nat_spec, tr_spec]
    scratch = []
    if fuse_retention:
        out_shape += [nat(BF16), jax.ShapeDtypeStruct((bx, H_RET, DK_RET, DV_RET), F32)]
        out_specs += [nat_spec, pl.BlockSpec((1, H_RET, DK_RET, DV_RET), lambda b, t: (b, 0, 0, 0))]
        scratch = [pltpu.VMEM((H_RET, DK_RET, DV_RET), F32)]
    else:
        out_shape += [nat(BF16)] * 4
        out_specs += [nat_spec] * 4
    return pl.pallas_call(
        functools.partial(_inproj_kernel, transposed=transposed, fuse_retention=fuse_retention),
        grid=(bx, nt),
        scratch_shapes=scratch,
        in_specs=[pl.BlockSpec((1, tm, d), lambda b, t: (b, t, 0)),
                  pl.BlockSpec((1, d), lambda b, t: (0, 0)),
                  _mod_spec(mod, d, MOD_SC1), _mod_spec(mod, d, MOD_SH1),
                  pl.BlockSpec((d, IN_COLS), lambda b, t: (0, 0)),
                  tab_spec, tab_spec, tab_spec, tab_spec],
        out_specs=out_specs,
        out_shape=out_shape,
        compiler_params=pltpu.CompilerParams(
            dimension_semantics=("arbitrary", "arbitrary"), vmem_limit_bytes=VMEM_LIMIT),
        name="inproj",
    )(x, g, mod, mod, w_in_b, *tables)


def _rope_tables(pos):
    lane = np.arange(V7X_LANES)
    pos = np.asarray(pos, np.float64)

    def one(d):
        half = d // 2
        inv = np.power(ROPE_THETA, -(2.0 / d) * np.arange(half, dtype=np.float64))
        ang = pos[:, None] * inv[None, :]
        idx = lane % half
        sign = np.where((lane % d) < half, -1.0, 1.0)
        return (jnp.asarray(np.cos(ang)[:, idx], F32), jnp.asarray(np.sin(ang)[:, idx] * sign[None, :], F32))

    cd, sd = one(DH_DIFF)
    cr, sr = one(DK_RET)
    return cd, sd, cr, sr


def _diff_lambda(lq1_ref, lk1_ref, lq2_ref, lk2_ref):
    e1 = jnp.exp(jnp.sum(lq1_ref[...] * lk1_ref[...], axis=-1, keepdims=True))
    e2 = jnp.exp(jnp.sum(lq2_ref[...] * lk2_ref[...], axis=-1, keepdims=True))
    return e1 - e2 + LAM_INIT


ATTN_SW = 256
ATTN_KB = 256
ATTN_TQ = 2 * ATTN_KB
ATTN_ONES_ROWS = 16


def _attn_kernel(qT_ref, k_ref, vT_ref, lq1_ref, lk1_ref, lq2_ref, lk2_ref, o_ref,
                 m_sc, l_sc, acc_sc, s_buf, p_buf, a_buf, *, tq):
    sw, kb = ATTN_SW, ATTN_KB
    assert tq == 2 * kb and tq % sw == 0
    ns = tq // sw
    nh = qT_ref.shape[1] // V7X_LANES
    qi = pl.program_id(2)
    zero = jnp.zeros((DH_DIFF, tq), BF16)
    q_sub = []
    for hh in range(nh):
        r0 = hh * V7X_LANES
        q_sub.append((jnp.concatenate([qT_ref[0, r0:r0 + DH_DIFF, :], zero], axis=0),
                      jnp.concatenate([zero, qT_ref[0, r0 + DH_DIFF:r0 + V7X_LANES, :]], axis=0)))
    m_sc[...] = jnp.full(m_sc.shape, NEG_BIG, F32)
    l_sc[...] = jnp.zeros(l_sc.shape, F32)
    acc_sc[...] = jnp.zeros(acc_sc.shape, F32)
    chains = [(hh, a, st) for hh in range(nh) for a in range(2) for st in range(ns)]
    chain_id = lambda hh, a, st: (hh * 2 + a) * ns + st
    head_lanes = lambda hh: slice(hh * V7X_LANES, (hh + 1) * V7X_LANES)

    def scores(slot, key_base, diag_step):
        key0 = pl.multiple_of(key_base, kb)
        for hh, a, st in chains:
            off = 0 if diag_step is None else diag_step * kb - st * sw
            if off > sw - 1:
                s_buf[slot, chain_id(hh, a, st)] = jnp.full((kb, sw), -jnp.inf, F32)
                continue
            kblk = k_ref[0, pl.ds(key0, kb), head_lanes(hh)]
            s = jnp.dot(kblk, q_sub[hh][a][:, st * sw:(st + 1) * sw], preferred_element_type=F32)
            if diag_step is not None:
                if off + kb - 1 > 0:
                    kpos = off + lax.broadcasted_iota(jnp.int32, (kb, sw), 0)
                    qpos = lax.broadcasted_iota(jnp.int32, (kb, sw), 1)
                    s = jnp.where(kpos <= qpos, s, -jnp.inf)
            s_buf[slot, chain_id(hh, a, st)] = s

    def softmax(slot):
        for hh, a, st in chains:
            c = chain_id(hh, a, st)
            m_old = m_sc[c]
            m_new = jnp.maximum(m_old, jnp.max(s_buf[slot, c], axis=0, keepdims=True))
            alpha = jnp.exp2(m_old - m_new)
            p = jnp.exp2(s_buf[slot, c] - m_new)
            m_sc[c] = m_new
            a_buf[slot, c] = alpha
            p_buf[slot, c] = p.astype(BF16)

    def values(slot, key_base):
        key0 = pl.multiple_of(key_base, kb)
        ones = jnp.ones((ATTN_ONES_ROWS, kb), BF16)
        for hh, a, st in chains:
            c = chain_id(hh, a, st)
            vblk = jnp.concatenate([vT_ref[0, head_lanes(hh), pl.ds(key0, kb)], ones], axis=0)
            pv = jnp.dot(vblk, p_buf[slot, c], preferred_element_type=F32)
            alpha = a_buf[slot, c]
            acc_sc[c] = acc_sc[c] * alpha + pv[:DV_DIFF]
            l_sc[c] = l_sc[c] * alpha + pv[DV_DIFF:DV_DIFF + 1]

    diag_base = [qi * tq + d * kb for d in range(2)]
    scores(0, diag_base[0], 0)
    scores(1, diag_base[1], 1)
    softmax(0)

    def trip_pair(u, carry):
        for par_t in range(2):
            full_base = (2 * u + par_t) * kb
            lag_base = jnp.where(u == 0, diag_base[par_t], full_base - 2 * kb)
            scores(par_t, full_base, None)
            softmax(1 - par_t)
            values(par_t, lag_base)
        return carry

    lax.fori_loop(0, qi, trip_pair, 0)
    last = [jnp.where(qi == 0, diag_base[d], (2 * qi - 2 + d) * kb) for d in range(2)]
    softmax(1)
    values(0, last[0])
    values(1, last[1])

    lam = _diff_lambda(lq1_ref, lk1_ref, lq2_ref, lk2_ref)
    for hh in range(nh):
        for st in range(ns):
            outs = [acc_sc[chain_id(hh, a, st)] / l_sc[chain_id(hh, a, st)] for a in range(2)]
            o = outs[0] - lam * outs[1]
            of = o * lax.rsqrt(jnp.mean(o * o, axis=0, keepdims=True) + NORM_EPS) * (1.0 - LAM_INIT)
            o_ref[0, st * sw:(st + 1) * sw, head_lanes(hh)] = of.T.astype(o_ref.dtype)


def _attn_prompt(qT, kb, vT, lams, *, tq, heads):
    b, _, t = qT.shape
    lam_spec = pl.BlockSpec((1, DH_DIFF), lambda b_, h, i: (0, 0))
    n_chains = heads * 2 * (tq // ATTN_SW)
    hw = heads * V7X_LANES
    return pl.pallas_call(
        functools.partial(_attn_kernel, tq=tq),
        grid=(b, H_DIFF // heads, t // tq),
        in_specs=[pl.BlockSpec((1, hw, tq), lambda b_, h, i: (b_, h, i)),
                  pl.BlockSpec((1, t, hw), lambda b_, h, i: (b_, 0, h)),
                  pl.BlockSpec((1, hw, t), lambda b_, h, i: (b_, h, 0)),
                  lam_spec, lam_spec, lam_spec, lam_spec],
        out_specs=pl.BlockSpec((1, tq, hw), lambda b_, h, i: (b_, i, h)),
        out_shape=jax.ShapeDtypeStruct((b, t, W_DIFF), BF16),
        scratch_shapes=[pltpu.VMEM((n_chains, 1, ATTN_SW), F32), pltpu.VMEM((n_chains, 1, ATTN_SW), F32),
                        pltpu.VMEM((n_chains, DV_DIFF, ATTN_SW), F32),
                        pltpu.VMEM((2, n_chains, ATTN_KB, ATTN_SW), F32),
                        pltpu.VMEM((2, n_chains, ATTN_KB, ATTN_SW), BF16),
                        pltpu.VMEM((2, n_chains, 1, ATTN_SW), F32)],
        compiler_params=pltpu.CompilerParams(
            dimension_semantics=("arbitrary", "arbitrary", "arbitrary"), vmem_limit_bytes=VMEM_LIMIT),
        name="attn_prompt",
    )(qT, kb, vT, *lams)


def _decode_kernel(pt_ref, q_ref, kn_ref, vn_ref, lq1_ref, lk1_ref, lq2_ref, lk2_ref, *rest, pages, ts):
    k_refs, v_refs = rest[:pages], rest[pages:2 * pages]
    o_ref, m_sc, l_sc, acc_sc = rest[2 * pages:]
    s_id = pl.program_id(1)
    q = q_ref[0]
    hr = 2 * ts

    @pl.when(s_id == 0)
    def _():
        m_sc[...] = jnp.full(m_sc.shape, NEG_BIG, F32)
        l_sc[...] = jnp.zeros(l_sc.shape, F32)
        acc_sc[...] = jnp.zeros(acc_sc.shape, F32)

    def update(s, v_heads):
        m_old = m_sc[:, 0:1]
        m_new = jnp.maximum(m_old, jnp.max(s, axis=1, keepdims=True))
        alpha = jnp.exp2(m_old - m_new)
        p = jnp.exp2(s - m_new)
        l_new = alpha * l_sc[:, 0:1] + jnp.sum(p, axis=1, keepdims=True)
        pb = p.astype(BF16)
        for h in range(H_DIFF):
            rows = slice(h * hr, (h + 1) * hr)
            pv = jnp.dot(pb[rows, :], v_heads[h], preferred_element_type=F32)
            acc_sc[rows, :] = acc_sc[rows, :] * alpha[rows, :] + pv
        m_sc[...] = jnp.broadcast_to(m_new, m_sc.shape)
        l_sc[...] = jnp.broadcast_to(l_new, l_sc.shape)

    kT = jnp.concatenate([kr[0].astype(BF16) for kr in k_refs], axis=1)
    s_past = jnp.dot(q, kT, preferred_element_type=F32)
    update(s_past, [jnp.concatenate([vr[0, pl.ds(h, PAGE_SIZE, stride=H_DIFF), :].astype(BF16)
                                     for vr in v_refs], axis=0) for h in range(H_DIFF)])

    @pl.when(s_id == pl.num_programs(1) - 1)
    def _():
        s_new = lax.dot_general(q, kn_ref[0], (((1,), (1,)), ((), ())), preferred_element_type=F32)
        key = lax.broadcasted_iota(jnp.int32, s_new.shape, 1)
        tok = lax.broadcasted_iota(jnp.int32, s_new.shape, 0) % ts
        s_new = jnp.where(key <= tok, s_new, NEG_BIG)
        update(s_new, [vn_ref[0, :, h * DV_DIFF:(h + 1) * DV_DIFF] for h in range(H_DIFF)])
        lam = _diff_lambda(lq1_ref, lk1_ref, lq2_ref, lk2_ref)
        on = acc_sc[...] / l_sc[:, 0:1]
        for h in range(H_DIFF):
            o = on[h * hr:h * hr + ts, :] - lam * on[h * hr + ts:(h + 1) * hr, :]
            of = o * lax.rsqrt(jnp.mean(o * o, axis=-1, keepdims=True) + NORM_EPS) * (1.0 - LAM_INIT)
            o_ref[0, :, h * DV_DIFF:(h + 1) * DV_DIFF] = of


def _attn_sample(q_rows, k_new, v_new, cache_k, cache_v, page_table, lams, *, pages, ts):
    bs, n_pages = page_table.shape
    rows = q_rows.shape[1]
    steps = n_pages // pages
    lam_spec = pl.BlockSpec((1, DH_DIFF), lambda b, s, pt: (0, 0))

    assert 2 * ts == 8, "one head's query rows must fill one sublane tile"

    def page_spec(i):
        return pl.BlockSpec((1, 512, V7X_LANES), lambda b, s, pt: (pt[b, s * pages + i], 0, 0))

    per_b = lambda r: pl.BlockSpec((1, r, 512), lambda b, s, pt: (b, 0, 0))
    grid_spec = pltpu.PrefetchScalarGridSpec(
        num_scalar_prefetch=1,
        grid=(bs, steps),
        in_specs=[per_b(rows), per_b(PAGE_SIZE), per_b(PAGE_SIZE), lam_spec, lam_spec, lam_spec, lam_spec]
                 + [page_spec(i) for i in range(pages)] + [page_spec(i) for i in range(pages)],
        out_specs=per_b(ts),
        scratch_shapes=[pltpu.VMEM((rows, V7X_LANES), F32), pltpu.VMEM((rows, V7X_LANES), F32),
                        pltpu.VMEM((rows, DV_DIFF), F32)],
    )
    return pl.pallas_call(
        functools.partial(_decode_kernel, pages=pages, ts=ts),
        grid_spec=grid_spec,
        out_shape=jax.ShapeDtypeStruct((bs, ts, W_DIFF), F32),
        compiler_params=pltpu.CompilerParams(
            dimension_semantics=("arbitrary", "arbitrary"), vmem_limit_bytes=VMEM_LIMIT),
        name="attn_sample",
    )(page_table, q_rows, k_new, v_new, *lams, *([cache_k] * pages), *([cache_v] * pages))


def _retention_units(get, state, h, put, *, chunk, chunk_len, n_chunks):
    ii = lax.broadcasted_iota(jnp.int32, (chunk, chunk), 0)
    jj = lax.broadcasted_iota(jnp.int32, (chunk, chunk), 1)
    dif = (ii - jj).astype(F32)
    tok = lax.broadcasted_iota(jnp.int32, (chunk, DK_RET), 0).astype(F32)
    decay = jnp.where(dif >= 0, jnp.exp(jnp.maximum(dif, 0.0) * LOG_G[h]), 0.0)
    q_dec = jnp.exp((tok + 1.0) * LOG_G[h])
    k_dec = jnp.exp((chunk_len - 1.0 - tok) * LOG_G[h])
    chunk_dec = math.exp(chunk_len * LOG_G[h])

    def unit(c):
        rows = slice(c * chunk, (c + 1) * chunk)
        qc, kc, vc = get(0, rows), get(1, rows), get(2, rows)
        r_prev = state[h]
        s = lax.dot_general(qc, kc, (((1,), (1,)), ((), ())), preferred_element_type=F32) * decay
        o = (jnp.dot(s.astype(qc.dtype), vc, preferred_element_type=F32)
             + jnp.dot(qc, r_prev.astype(qc.dtype), preferred_element_type=F32) * q_dec)
        vd = (vc.astype(F32) * k_dec).astype(qc.dtype)
        kv = lax.dot_general(kc, vd, (((0,), (0,)), ((), ())), preferred_element_type=F32)
        state[h] = r_prev * chunk_dec + kv
        mu = jnp.mean(o, axis=-1, keepdims=True)
        var = jnp.mean(jnp.square(o - mu), axis=-1, keepdims=True)
        nrm = (o - mu) * lax.rsqrt(var + GN_EPS)
        put(rows, nrm * _silu(get(3, rows).astype(F32)))

    return [functools.partial(unit, c) for c in range(n_chunks)]


def _retention_kernel(q_ref, k_ref, v_ref, g_ref, r0_ref, o_ref, rout_ref, state, *, chunk, chunk_len, n_chunks):
    t = pl.program_id(1)
    refs = (q_ref, k_ref, v_ref, g_ref)

    @pl.when(t == 0)
    def _():
        state[...] = r0_ref[...]

    for bi in range(q_ref.shape[0]):
        for h in range(H_RET):
            cols = slice(h * DK_RET, (h + 1) * DK_RET)

            def get(i, rows, bi=bi, cols=cols):
                return refs[i][bi, rows, cols]

            def put(rows, val, bi=bi, cols=cols):
                o_ref[bi, rows, cols] = val.astype(o_ref.dtype)

            for unit in _retention_units(get, state.at[bi], h, put, chunk=chunk, chunk_len=chunk_len,
                                         n_chunks=n_chunks):
                unit()

    @pl.when(t == pl.num_programs(1) - 1)
    def _():
        rout_ref[...] = state[...]


def _retention(qr, kr, vr, gr, r0, *, chunk, chunk_len, n_chunks, seqs):
    b, t, _ = qr.shape
    tr = chunk * n_chunks
    tok_spec = pl.BlockSpec((seqs, tr, 512), lambda b_, i: (b_, i, 0))
    st_spec = pl.BlockSpec((seqs, H_RET, DK_RET, DV_RET), lambda b_, i: (b_, 0, 0, 0))
    return pl.pallas_call(
        functools.partial(_retention_kernel, chunk=chunk, chunk_len=chunk_len, n_chunks=n_chunks),
        grid=(b // seqs, t // tr),
        in_specs=[tok_spec] * 4 + [st_spec],
        out_specs=[tok_spec, st_spec],
        out_shape=[jax.ShapeDtypeStruct((b, t, W_RET), BF16),
                   jax.ShapeDtypeStruct((b, H_RET, DK_RET, DV_RET), F32)],
        scratch_shapes=[pltpu.VMEM((seqs, H_RET, DK_RET, DV_RET), F32)],
        compiler_params=pltpu.CompilerParams(
            dimension_semantics=("arbitrary", "arbitrary"), vmem_limit_bytes=VMEM_LIMIT),
        name="retention",
    )(qr, kr, vr, gr, r0)


def _route(z):
    lane = lax.broadcasted_iota(jnp.int32, z.shape, 1).astype(F32)
    big = 1e9
    neg = -jnp.inf
    is_g = lane < N_GROUPS
    glog = jnp.where(is_g, z, neg)
    gmax = jnp.max(glog, axis=-1, keepdims=True)
    gsel = jnp.min(jnp.where(glog == gmax, lane, big), axis=-1, keepdims=True)
    psum = jnp.sum(jnp.where(is_g, jnp.exp(z - gmax), 0.0), axis=-1, keepdims=True)
    pg_sel = 1.0 / psum
    rel = jnp.zeros_like(z)
    for g in range(N_GROUPS):
        first = ROUTER_E0 + g * EXPERTS_PER_GROUP
        rel = jnp.where(gsel == float(g), pltpu.roll(z, V7X_LANES - first, 1), rel)
    elog = jnp.where(lane < EXPERTS_PER_GROUP, rel, neg)
    v1 = jnp.max(elog, axis=-1, keepdims=True)
    i1 = jnp.min(jnp.where(elog == v1, lane, big), axis=-1, keepdims=True)
    elog2 = jnp.where(lane == i1, neg, elog)
    v2 = jnp.max(elog2, axis=-1, keepdims=True)
    i2 = jnp.min(jnp.where(elog2 == v2, lane, big), axis=-1, keepdims=True)
    e2 = jnp.exp(v2 - v1)
    den = 1.0 + e2
    w1 = (1.0 / den) * pg_sel
    w2 = (e2 / den) * pg_sel
    return gsel, jnp.where(lane == i1, w1, jnp.where(lane == i2, w2, 0.0))


MOE_CHUNK = 160
MOE_PAD = 256


def _ffn_kernel(x_ref, od_ref, rf_ref, beta_ref, wout_ref, gt1_ref, gffn_ref, sc2_ref, sh2_ref, gt2_ref,
                wr_ref, br_ref, tri_ref, wgu_ref, wd_ref, gfin_ref, y_ref,
                h2_sc, pos_sc, posr_sc, moe_sc, *, chunk):
    tm = x_ref.shape[1]
    beta = beta_ref[...]
    yd = (od_ref[0].astype(F32) * beta[:, :W_DIFF]).astype(BF16)
    yr = (rf_ref[0].astype(F32) * beta[:, W_DIFF:]).astype(BF16)
    mix = (jnp.dot(yd, wout_ref[:W_DIFF, :], preferred_element_type=F32)
           + jnp.dot(yr, wout_ref[W_DIFF:, :], preferred_element_type=F32))
    x1 = x_ref[0] + gt1_ref[0] * mix
    n2 = x1 * lax.rsqrt(jnp.mean(x1 * x1, axis=-1, keepdims=True) + NORM_EPS)
    h2 = (n2 * gffn_ref[...]) * (1.0 + sc2_ref[0]) + sh2_ref[0]
    h_hi = h2.astype(BF16)
    h_lo = (h2 - h_hi.astype(F32)).astype(BF16)
    zz = jnp.dot(h_hi, wr_ref[...], preferred_element_type=F32)
    z = (zz[:, :V7X_LANES] + zz[:, V7X_LANES:]
         + jnp.dot(h_lo, wr_ref[:, :V7X_LANES], preferred_element_type=F32) + br_ref[...])
    gsel, gates = _route(z)
    d_model = h2.shape[1]
    g_hi = gates.astype(BF16)
    h2_sc[:, :d_model] = h_hi
    h2_sc[:, d_model:d_model + V7X_LANES] = g_hi
    h2_sc[:, d_model + V7X_LANES:] = (gates - g_hi.astype(F32)).astype(BF16)

    lane = lax.broadcasted_iota(jnp.int32, (tm, V7X_LANES), 1).astype(F32)
    lane1 = lax.broadcasted_iota(jnp.int32, (1, V7X_LANES), 1)
    onehot = jnp.where(lane == gsel, 1.0, 0.0)
    earlier = jnp.dot(tri_ref[...], onehot.astype(BF16), preferred_element_type=F32)
    rank = jnp.sum(onehot * earlier, axis=-1, keepdims=True)
    cnt = jnp.sum(onehot, axis=0, keepdims=True)
    n_chunks = jnp.zeros_like(cnt)
    for m in range(-(-tm // chunk)):
        n_chunks = n_chunks + jnp.where(cnt > m * chunk, 1.0, 0.0)
    first_chunk = [jnp.int32(0)]
    for g in range(N_GROUPS):
        first_chunk.append(first_chunk[-1] + n_chunks[0, g].astype(jnp.int32))
    first_vec = jnp.zeros((1, V7X_LANES), F32)
    for g in range(N_GROUPS):
        first_vec = jnp.where(lane1 == g, first_chunk[g].astype(F32), first_vec)
    pos = jnp.sum(onehot * first_vec, axis=-1, keepdims=True) * chunk + rank
    pos_sc[...] = jnp.broadcast_to(pos, pos_sc.shape)
    posr_sc[...] = jnp.broadcast_to(pos, (tm, V7X_LANES)).T[0:8, :]
    moe_sc[...] = jnp.zeros(moe_sc.shape, F32)

    col_id = lax.broadcasted_iota(jnp.int32, (tm, MOE_PAD), 1)
    col_f = col_id.astype(F32)
    row_f = lax.broadcasted_iota(jnp.int32, (chunk, tm), 0).astype(F32)

    def chunk_body(k, carry):
        g = (jnp.where(k >= first_chunk[1], 1, 0) + jnp.where(k >= first_chunk[2], 1, 0)
             + jnp.where(k >= first_chunk[3], 1, 0))
        off = (k * chunk).astype(F32)
        gather = jnp.where(posr_sc[0:1, :] - off == row_f, 1.0, 0.0).astype(BF16)
        scatter = jnp.where((pos_sc[...] - off == col_f) & (col_id < chunk), 1.0, 0.0).astype(BF16)
        gathered = jnp.dot(gather, h2_sc[...], preferred_element_type=F32)
        hc = gathered[:, :d_model].astype(BF16)
        gc = gathered[:, d_model:d_model + V7X_LANES] + gathered[:, d_model + V7X_LANES:]
        hh = []
        for e4 in range(EXPERTS_PER_GROUP):
            au = jnp.dot(hc, wgu_ref[g * EXPERTS_PER_GROUP + e4], preferred_element_type=F32)
            a, u = au[:, :D_EXPERT], au[:, D_EXPERT:]
            hh.append((_silu(a) * u * gc[:, e4:e4 + 1]).astype(BF16))
        oc = jnp.dot(jnp.concatenate(hh, axis=1), wd_ref[g], preferred_element_type=F32)
        oc = jnp.concatenate([oc.astype(BF16), jnp.zeros((MOE_PAD - chunk, oc.shape[1]), BF16)], axis=0)
        moe_sc[...] += jnp.dot(scatter, oc, preferred_element_type=F32)
        return carry

    lax.fori_loop(0, first_chunk[N_GROUPS], chunk_body, 0)
    x2 = x1 + gt2_ref[0] * moe_sc[...]
    y_ref[0] = (x2 * lax.rsqrt(jnp.mean(x2 * x2, axis=-1, keepdims=True) + NORM_EPS)) * gfin_ref[...]


def _ffn(x, od, rf, mod, weights, *, tm):
    bx, tx, d = x.shape
    beta, w_out_b, g_ffn, w_router, b_router, wgu_b, wd_b, g_fin = weights
    chunk = min(MOE_CHUNK, tm)
    tri = jnp.asarray(np.tril(np.ones((tm, tm), np.float32), -1), BF16)
    const = lambda shape: pl.BlockSpec(shape, lambda b, t: (0,) * len(shape), pipeline_mode=pl.Buffered(1))
    tok = lambda w: pl.BlockSpec((1, tm, w), lambda b, t: (b, t, 0))
    gw = EXPERTS_PER_GROUP * D_EXPERT
    return pl.pallas_call(
        functools.partial(_ffn_kernel, chunk=chunk),
        grid=(bx, tx // tm),
        in_specs=[tok(d), tok(W_DIFF), tok(W_RET), const((1, d)), const((d, d)), _mod_spec(mod, d, MOD_GT1),
                  const((1, d)), _mod_spec(mod, d, MOD_SC2), _mod_spec(mod, d, MOD_SH2), _mod_spec(mod, d, MOD_GT2),
                  const((d, 2 * V7X_LANES)), const((1, V7X_LANES)),
                  const((tm, tm)), const((N_EXPERTS, d, 2 * D_EXPERT)), const((N_GROUPS, gw, d)), const((1, d))],
        out_specs=tok(d),
        out_shape=jax.ShapeDtypeStruct((bx, tx, d), F32),
        scratch_shapes=[pltpu.VMEM((tm, d + 2 * V7X_LANES), BF16),
                        pltpu.VMEM((tm, MOE_PAD), F32), pltpu.VMEM((8, tm), F32),
                        pltpu.VMEM((tm, d), F32)],
        compiler_params=pltpu.CompilerParams(
            dimension_semantics=("arbitrary", "arbitrary"), vmem_limit_bytes=VMEM_LIMIT),
        name="ffn",
    )(x, od, rf, beta, w_out_b, mod, g_ffn, mod, mod, mod, w_router, b_router, tri, wgu_b, wd_b, g_fin)


def kernel(x_prompt, x_sample, cache_k, cache_v, state_ret, page_table, c_prompt, c_sample, w_ada, b_ada,
           norm_mix_g, norm_ffn_g, w_in, lambda_q1, lambda_k1, lambda_q2, lambda_k2, beta_mix, w_out,
           w_group, b_group, w_expert_router, b_expert_router, w_gate_e, w_up_e, w_down_e, final_g):
    assert w_ada.shape[0] == 1, "single-layer stack only"
    bp, tp, d = x_prompt.shape
    bs, ts, _ = x_sample.shape
    n_pages = page_table.shape[1]
    past_len = n_pages * PAGE_SIZE
    n_phys = cache_k.shape[1]

    w_in_b = w_in[0].astype(BF16)
    w_out_b = w_out[0].astype(BF16)
    wgu_b = jnp.concatenate([w_gate_e[0], w_up_e[0]], axis=-1).astype(BF16)
    wd_b = w_down_e[0].reshape(N_GROUPS, EXPERTS_PER_GROUP * D_EXPERT, d).astype(BF16)
    w_er = jnp.transpose(w_expert_router[0], (1, 0, 2)).reshape(d, N_EXPERTS)
    pad = V7X_LANES - N_GROUPS - N_EXPERTS
    w_router = jnp.concatenate([w_group[0], w_er, jnp.zeros((d, pad), F32)], axis=1)
    w_router_hi = w_router.astype(BF16)
    w_router = jnp.concatenate([w_router_hi, (w_router - w_router_hi.astype(F32)).astype(BF16)], axis=1)
    b_router = jnp.concatenate([b_group[0], b_expert_router[0].reshape(N_EXPERTS), jnp.zeros((pad,), F32)])[None]
    ffn_weights = (beta_mix, w_out_b, norm_ffn_g, w_router, b_router, wgu_b, wd_b, final_g[None])
    lams = (lambda_q1, lambda_k1, lambda_q2, lambda_k2)

    mod = _adaln(jnp.concatenate([c_prompt, c_sample], axis=0), w_ada[0], b_ada[0])
    mod_p = mod[:bp, None, :]
    mod_s = jnp.repeat(mod[bp:], ts, axis=0)[None]

    tabs_p = _rope_tables(np.arange(tp))
    kp, vp, qT, kb, vT, rf_p, ret_p = _inproj(
        x_prompt, norm_mix_g, mod_p, w_in_b, tabs_p, tm=512, transposed=True, fuse_retention=True)
    od_p = _attn_prompt(qT, kb, vT, lams, tq=ATTN_TQ, heads=2)
    y_p = _ffn(x_prompt, od_p, rf_p, mod_p, ffn_weights, tm=512)

    rows_s = bs * ts
    tabs_s = _rope_tables(np.tile(past_len + np.arange(ts), bs))
    ks, vs, q_s, kb_s, vb_s, qr_s, kr_s, vr_s, gr_s = _inproj(
        x_sample.reshape(1, rows_s, d), norm_mix_g, mod_s, w_in_b, tabs_s, tm=rows_s, transposed=False,
        fuse_retention=False)
    sub = jnp.arange(2 * H_DIFF)
    col_sub = jnp.arange(512) // DH_DIFF
    q_rows = jnp.where((sub[:, None, None] == col_sub[None, None, :])[None],
                       q_s.reshape(bs, 1, ts, 512), jnp.zeros((), BF16)).reshape(bs, 2 * H_DIFF * ts, 512)
    pad_keys = lambda a: jnp.pad(a.reshape(bs, ts, 512), ((0, 0), (0, PAGE_SIZE - ts), (0, 0)))
    ck = jnp.transpose(cache_k[0], (0, 2, 3, 1)).reshape(n_phys, 2 * H_DIFF * DH_DIFF, PAGE_SIZE)
    cv = cache_v[0].reshape(n_phys, PAGE_SIZE * H_DIFF, DV_DIFF)
    od_s = _attn_sample(q_rows, pad_keys(kb_s), pad_keys(vb_s), ck, cv, page_table, lams, pages=32, ts=ts)
    chunk_s = 16
    pad_tok = lambda a: jnp.pad(a.reshape(bs, ts, 512), ((0, 0), (0, chunk_s - ts), (0, 0)))
    rf_s, ret_s = _retention(pad_tok(qr_s), pad_tok(kr_s), pad_tok(vr_s), pad_tok(gr_s), state_ret[0],
                             chunk=chunk_s, chunk_len=ts, n_chunks=1, seqs=8)
    y_s = _ffn(x_sample.reshape(1, rows_s, d), od_s.reshape(1, rows_s, 512),
               rf_s[:, :ts].reshape(1, rows_s, 512), mod_s, ffn_weights, tm=rows_s)

    return (y_p, y_s.reshape(bs, ts, d),
            kp.reshape(1, bp, tp, 2 * H_DIFF, DH_DIFF), vp.reshape(1, bp, tp, H_DIFF, DV_DIFF), ret_p[None],
            ks.reshape(1, bs, ts, 2 * H_DIFF, DH_DIFF), vs.reshape(1, bs, ts, H_DIFF, DV_DIFF), ret_s[None])
```

```python
import functools
import math

import jax
import jax.numpy as jnp
import numpy as np
from jax import lax
from jax.experimental import pallas as pl
from jax.experimental.pallas import tpu as pltpu

F32 = jnp.float32
BF16 = jnp.bfloat16

V7X_LANES = 128
V7X_VMEM_BYTES = 64 * 1024 * 1024
VMEM_LIMIT = V7X_VMEM_BYTES * 7 // 8

D_MODEL = 1024
PAGE_SIZE = 128
H_DIFF = 4
DH_DIFF = 64
DV_DIFF = 128
W_DIFF = H_DIFF * DV_DIFF
H_RET = 4
DK_RET = 128
DV_RET = 128
W_RET = H_RET * DV_RET
W_SEC = 512
assert W_SEC == 2 * H_DIFF * DH_DIFF == W_DIFF == H_RET * DK_RET == W_RET
IN_COLS = 7 * W_SEC
TM_PROMPT = 512
DECODE_PAGES = 32
ATTN_HEADS = 2
SAMPLE_RET_SEQS = 8
SAMPLE_RET_ROWS = 16
RET_CHUNK = 256
ROPE_THETA = 10000.0
N_GROUPS = 4
EXPERTS_PER_GROUP = 4
N_EXPERTS = 16
D_EXPERT = 256
NORM_EPS = 1e-6
GN_EPS = 1e-5
LAM_INIT = 0.8 - 0.6 * math.exp(-0.3 * 0)
LOG2E = 1.4426950408889634
NEG_BIG = -1e30
LOG_G = tuple(math.log1p(-2.0 ** (-5.0 - h)) for h in range(H_RET))
ROUTER_E0 = N_GROUPS


def _silu(x):
    return x * jax.nn.sigmoid(x)


def _adaln_kernel(c_ref, w_ref, b_ref, o_ref):
    s = _silu(c_ref[...])
    o_ref[...] = jnp.dot(s, w_ref[...], preferred_element_type=F32,
                         precision=lax.Precision.HIGHEST) + b_ref[...]


def _adaln(c, w_ada, b_ada):
    n, d = c.shape
    cols = w_ada.shape[1]
    bn = 1024
    return pl.pallas_call(
        _adaln_kernel,
        grid=(cols // bn,),
        in_specs=[pl.BlockSpec((n, d), lambda j: (0, 0)),
                  pl.BlockSpec((d, bn), lambda j: (0, j)),
                  pl.BlockSpec((1, bn), lambda j: (0, j))],
        out_specs=pl.BlockSpec((n, bn), lambda j: (0, j)),
        out_shape=jax.ShapeDtypeStruct((n, cols), F32),
        name="adaln",
    )(c, w_ada, b_ada.reshape(1, cols))


def _rope64(z, cos, sin_signed, first_half):
    rot = jnp.where(first_half, pltpu.roll(z, 96, 1), pltpu.roll(z, 32, 1))
    return z * cos + rot * sin_signed


def _rope128(z, cos, sin_signed):
    return z * cos + pltpu.roll(z, 64, 1) * sin_signed


def _inproj_kernel(x_ref, g_ref, sc_ref, sh_ref, w_ref, cd_ref, sd_ref, cr_ref, sr_ref,
                   kout_ref, vout_ref, q_ref, kb_ref, v_ref, *rest, transposed, fuse_retention):
    tm = x_ref.shape[1]
    if fuse_retention:
        rf_ref, rout_ref, state = rest
        t = pl.program_id(1)

        @pl.when(t == 0)
        def _():
            state[...] = jnp.zeros(state.shape, F32)
    else:
        qr_ref, kr_ref, vr_ref, gr_ref = rest
    x = x_ref[0]
    y = x * lax.rsqrt(jnp.mean(x * x, axis=-1, keepdims=True) + NORM_EPS)
    h = (y * g_ref[...]) * (1.0 + sc_ref[0]) + sh_ref[0]
    hb = h.astype(BF16)

    def proj(section):
        return jnp.dot(hb, w_ref[:, section * W_SEC:(section + 1) * W_SEC], preferred_element_type=F32)

    zq, zk, zv, zqr, zkr, zvr, zgr = [proj(i) for i in range(7)]

    cd, sd, cr, sr = cd_ref[...], sd_ref[...], cr_ref[...], sr_ref[...]
    lane = lax.broadcasted_iota(jnp.int32, (tm, V7X_LANES), 1)
    first_half = (lane % DH_DIFF) < (DH_DIFF // 2)
    q_scale = (DH_DIFF ** -0.5) * LOG2E
    for j in range(4):
        cols = slice(j * V7X_LANES, (j + 1) * V7X_LANES)
        q = _rope64(zq[:, cols], cd, sd, first_half) * q_scale
        k = _rope64(zk[:, cols], cd, sd, first_half)
        v = zv[:, cols]
        kout_ref[0, :, cols] = k
        vout_ref[0, pl.ds(j, tm, stride=H_DIFF), :] = v
        kb_ref[0, :, cols] = k.astype(BF16)
        if transposed:
            q_ref[0, cols, :] = q.T.astype(BF16)
            v_ref[0, cols, :] = v.T.astype(BF16)
        else:
            q_ref[0, :, cols] = q.astype(BF16)
            v_ref[0, :, cols] = v.astype(BF16)
        qr = _rope128(zqr[:, cols], cr, sr).astype(BF16)
        kr = (_rope128(zkr[:, cols], cr, sr) * (DK_RET ** -0.5)).astype(BF16)
        vr = zvr[:, cols].astype(BF16)
        gr = zgr[:, cols].astype(BF16)
        if fuse_retention:
            operands = (qr, kr, vr, gr)

            def get(i, rows, operands=operands):
                return operands[i][rows]

            def put(rows, val, cols=cols):
                rf_ref[0, rows, cols] = val.astype(rf_ref.dtype)

            for unit in _retention_units(get, state, j, put, chunk=RET_CHUNK, chunk_len=RET_CHUNK,
                                         n_chunks=tm // RET_CHUNK):
                unit()
        else:
            qr_ref[0, :, cols] = qr
            kr_ref[0, :, cols] = kr
            vr_ref[0, :, cols] = vr
            gr_ref[0, :, cols] = gr

    if fuse_retention:
        @pl.when(t == pl.num_programs(1) - 1)
        def _():
            rout_ref[0] = state[...]


def _mod_spec(mod, d, col):
    r = mod.shape[1]
    if r == 1:
        return pl.BlockSpec((1, 1, d), lambda b, t: (b, 0, col))
    return pl.BlockSpec((1, r, d), lambda b, t: (b, t, col))


MOD_SH1, MOD_SC1, MOD_GT1, MOD_SH2, MOD_SC2, MOD_GT2 = range(6)


def _inproj(x, g, mod, w_in_b, tables, *, tm, transposed, fuse_retention):
    bx, tx, d = x.shape
    nt = tx // tm
    tab_spec = pl.BlockSpec((tm, V7X_LANES), lambda b, t: (t, 0))
    nat = lambda dt: jax.ShapeDtypeStruct((bx, tx, W_SEC), dt)
    nat_spec = pl.BlockSpec((1, tm, W_SEC), lambda b, t: (b, t, 0))
    if transposed:
        tr = jax.ShapeDtypeStruct((bx, W_SEC, tx), BF16)
        tr_spec = pl.BlockSpec((1, W_SEC, tm), lambda b, t: (b, 0, t))
    else:
        tr, tr_spec = nat(BF16), nat_spec
    vrows = jax.ShapeDtypeStruct((bx, tx * H_DIFF, DV_DIFF), F32)
    vrows_spec = pl.BlockSpec((1, tm * H_DIFF, DV_DIFF), lambda b, t: (b, t, 0))
    out_shape = [nat(F32), vrows, tr, nat(BF16), tr]
    out_specs = [nat_spec, vrows_spec, tr_spec, nat_spec, tr_spec]
    scratch = []
    if fuse_retention:
        out_shape += [nat(BF16), jax.ShapeDtypeStruct((bx, H_RET, DK_RET, DV_RET), F32)]
        out_specs += [nat_spec, pl.BlockSpec((1, H_RET, DK_RET, DV_RET), lambda b, t: (b, 0, 0, 0))]
        scratch = [pltpu.VMEM((H_RET, DK_RET, DV_RET), F32)]
    else:
        out_shape += [nat(BF16)] * 4
        out_specs += [nat_spec] * 4
    return pl.pallas_call(
        functools.partial(_inproj_kernel, transposed=transposed, fuse_retention=fuse_retention),
        grid=(bx, nt),
        scratch_shapes=scratch,
        in_specs=[pl.BlockSpec((1, tm, d), lambda b, t: (b, t, 0)),
                  pl.BlockSpec((1, d), lambda b, t: (0, 0)),
                  _mod_spec(mod, d, MOD_SC1), _mod_spec(mod, d, MOD_SH1),
                  pl.BlockSpec((d, IN_COLS), lambda b, t: (0, 0)),
                  tab_spec, tab_spec, tab_spec, tab_spec],
        out_specs=out_specs,
        out_shape=out_shape,
        compiler_params=pltpu.CompilerParams(
            dimension_semantics=("arbitrary", "arbitrary"), vmem_limit_bytes=VMEM_LIMIT),
        name="inproj",
    )(x, g, mod, mod, w_in_b, *tables)


def _rope_tables(pos):
    lane = np.arange(V7X_LANES)
    pos = np.asarray(pos, np.float64)

    def one(d):
        half = d // 2
        inv = np.power(ROPE_THETA, -(2.0 / d) * np.arange(half, dtype=np.float64))
        ang = pos[:, None] * inv[None, :]
        idx = lane % half
        sign = np.where((lane % d) < half, -1.0, 1.0)
        return (jnp.asarray(np.cos(ang)[:, idx], F32), jnp.asarray(np.sin(ang)[:, idx] * sign[None, :], F32))

    cd, sd = one(DH_DIFF)
    cr, sr = one(DK_RET)
    return cd, sd, cr, sr


def _diff_lambda(lq1_ref, lk1_ref, lq2_ref, lk2_ref):
    e1 = jnp.exp(jnp.sum(lq1_ref[...] * lk1_ref[...], axis=-1, keepdims=True))
    e2 = jnp.exp(jnp.sum(lq2_ref[...] * lk2_ref[...], axis=-1, keepdims=True))
    return e1 - e2 + LAM_INIT


ATTN_SW = 256
ATTN_KB = 256
ATTN_TQ = 2 * ATTN_KB
ATTN_ONES_ROWS = 16


def _attn_kernel(qT_ref, k_ref, vT_ref, lq1_ref, lk1_ref, lq2_ref, lk2_ref, o_ref,
                 m_sc, l_sc, acc_sc, s_buf, p_buf, a_buf, *, tq):
    sw, kb = ATTN_SW, ATTN_KB
    assert tq == 2 * kb and tq % sw == 0
    ns = tq // sw
    nh = qT_ref.shape[1] // V7X_LANES
    qi = pl.program_id(2)
    zero = jnp.zeros((DH_DIFF, tq), BF16)
    q_sub = []
    for hh in range(nh):
        r0 = hh * V7X_LANES
        q_sub.append((jnp.concatenate([qT_ref[0, r0:r0 + DH_DIFF, :], zero], axis=0),
                      jnp.concatenate([zero, qT_ref[0, r0 + DH_DIFF:r0 + V7X_LANES, :]], axis=0)))
    m_sc[...] = jnp.full(m_sc.shape, NEG_BIG, F32)
    l_sc[...] = jnp.zeros(l_sc.shape, F32)
    acc_sc[...] = jnp.zeros(acc_sc.shape, F32)
    chains = [(hh, a, st) for hh in range(nh) for a in range(2) for st in range(ns)]
    chain_id = lambda hh, a, st: (hh * 2 + a) * ns + st
    head_lanes = lambda hh: slice(hh * V7X_LANES, (hh + 1) * V7X_LANES)

    def scores(slot, key_base, diag_step):
        key0 = pl.multiple_of(key_base, kb)
        for hh, a, st in chains:
            off = 0 if diag_step is None else diag_step * kb - st * sw
            if off > sw - 1:
                s_buf[slot, chain_id(hh, a, st)] = jnp.full((kb, sw), -jnp.inf, F32)
                continue
            kblk = k_ref[0, pl.ds(key0, kb), head_lanes(hh)]
            s = jnp.dot(kblk, q_sub[hh][a][:, st * sw:(st + 1) * sw], preferred_element_type=F32)
            if diag_step is not None:
                if off + kb - 1 > 0:
                    kpos = off + lax.broadcasted_iota(jnp.int32, (kb, sw), 0)
                    qpos = lax.broadcasted_iota(jnp.int32, (kb, sw), 1)
                    s = jnp.where(kpos <= qpos, s, -jnp.inf)
            s_buf[slot, chain_id(hh, a, st)] = s

    def softmax(slot):
        for hh, a, st in chains:
            c = chain_id(hh, a, st)
            m_old = m_sc[c]
            m_new = jnp.maximum(m_old, jnp.max(s_buf[slot, c], axis=0, keepdims=True))
            alpha = jnp.exp2(m_old - m_new)
            p = jnp.exp2(s_buf[slot, c] - m_new)
            m_sc[c] = m_new
            a_buf[slot, c] = alpha
            p_buf[slot, c] = p.astype(BF16)

    def values(slot, key_base):
        key0 = pl.multiple_of(key_base, kb)
        ones = jnp.ones((ATTN_ONES_ROWS, kb), BF16)
        for hh, a, st in chains:
            c = chain_id(hh, a, st)
            vblk = jnp.concatenate([vT_ref[0, head_lanes(hh), pl.ds(key0, kb)], ones], axis=0)
            pv = jnp.dot(vblk, p_buf[slot, c], preferred_element_type=F32)
            alpha = a_buf[slot, c]
            acc_sc[c] = acc_sc[c] * alpha + pv[:DV_DIFF]
            l_sc[c] = l_sc[c] * alpha + pv[DV_DIFF:DV_DIFF + 1]

    diag_base = [qi * tq + d * kb for d in range(2)]
    scores(0, diag_base[0], 0)
    scores(1, diag_base[1], 1)
    softmax(0)

    def trip_pair(u, carry):
        for par_t in range(2):
            full_base = (2 * u + par_t) * kb
            lag_base = jnp.where(u == 0, diag_base[par_t], full_base - 2 * kb)
            scores(par_t, full_base, None)
            softmax(1 - par_t)
            values(par_t, lag_base)
        return carry

    lax.fori_loop(0, qi, trip_pair, 0)
    last = [jnp.where(qi == 0, diag_base[d], (2 * qi - 2 + d) * kb) for d in range(2)]
    softmax(1)
    values(0, last[0])
    values(1, last[1])

    lam = _diff_lambda(lq1_ref, lk1_ref, lq2_ref, lk2_ref)
    for hh in range(nh):
        for st in range(ns):
            outs = [acc_sc[chain_id(hh, a, st)] / l_sc[chain_id(hh, a, st)] for a in range(2)]
            o = outs[0] - lam * outs[1]
            of = o * lax.rsqrt(jnp.mean(o * o, axis=0, keepdims=True) + NORM_EPS) * (1.0 - LAM_INIT)
            o_ref[0, st * sw:(st + 1) * sw, head_lanes(hh)] = of.T.astype(o_ref.dtype)


def _attn_prompt(qT, kb, vT, lams, *, tq, heads):
    b, _, t = qT.shape
    lam_spec = pl.BlockSpec((1, DH_DIFF), lambda b_, h, i: (0, 0))
    n_chains = heads * 2 * (tq // ATTN_SW)
    hw = heads * V7X_LANES
    return pl.pallas_call(
        functools.partial(_attn_kernel, tq=tq),
        grid=(b, H_DIFF // heads, t // tq),
        in_specs=[pl.BlockSpec((1, hw, tq), lambda b_, h, i: (b_, h, i)),
                  pl.BlockSpec((1, t, hw), lambda b_, h, i: (b_, 0, h)),
                  pl.BlockSpec((1, hw, t), lambda b_, h, i: (b_, h, 0)),
                  lam_spec, lam_spec, lam_spec, lam_spec],
        out_specs=pl.BlockSpec((1, tq, hw), lambda b_, h, i: (b_, i, h)),
        out_shape=jax.ShapeDtypeStruct((b, t, W_DIFF), BF16),
        scratch_shapes=[pltpu.VMEM((n_chains, 1, ATTN_SW), F32), pltpu.VMEM((n_chains, 1, ATTN_SW), F32),
                        pltpu.VMEM((n_chains, DV_DIFF, ATTN_SW), F32),
                        pltpu.VMEM((2, n_chains, ATTN_KB, ATTN_SW), F32),
                        pltpu.VMEM((2, n_chains, ATTN_KB, ATTN_SW), BF16),
                        pltpu.VMEM((2, n_chains, 1, ATTN_SW), F32)],
        compiler_params=pltpu.CompilerParams(
            dimension_semantics=("arbitrary", "arbitrary", "arbitrary"), vmem_limit_bytes=VMEM_LIMIT),
        name="attn_prompt",
    )(qT, kb, vT, *lams)


def _decode_kernel(pt_ref, q_ref, kn_ref, vn_ref, lq1_ref, lk1_ref, lq2_ref, lk2_ref, *rest, pages, ts):
    k_refs, v_refs = rest[:pages], rest[pages:2 * pages]
    o_ref, m_sc, l_sc, acc_sc = rest[2 * pages:]
    s_id = pl.program_id(1)
    q = q_ref[0]
    hr = 2 * ts

    @pl.when(s_id == 0)
    def _():
        m_sc[...] = jnp.full(m_sc.shape, NEG_BIG, F32)
        l_sc[...] = jnp.zeros(l_sc.shape, F32)
        acc_sc[...] = jnp.zeros(acc_sc.shape, F32)

    def update(s, v_heads):
        m_old = m_sc[:, 0:1]
        m_new = jnp.maximum(m_old, jnp.max(s, axis=1, keepdims=True))
        alpha = jnp.exp2(m_old - m_new)
        p = jnp.exp2(s - m_new)
        l_new = alpha * l_sc[:, 0:1] + jnp.sum(p, axis=1, keepdims=True)
        pb = p.astype(BF16)
        for h in range(H_DIFF):
            rows = slice(h * hr, (h + 1) * hr)
            pv = jnp.dot(pb[rows, :], v_heads[h], preferred_element_type=F32)
            acc_sc[rows, :] = acc_sc[rows, :] * alpha[rows, :] + pv
        m_sc[...] = jnp.broadcast_to(m_new, m_sc.shape)
        l_sc[...] = jnp.broadcast_to(l_new, l_sc.shape)

    kT = jnp.concatenate([kr[0].astype(BF16) for kr in k_refs], axis=1)
    s_past = jnp.dot(q, kT, preferred_element_type=F32)
    update(s_past, [jnp.concatenate([vr[0, pl.ds(h, PAGE_SIZE, stride=H_DIFF), :].astype(BF16)
                                     for vr in v_refs], axis=0) for h in range(H_DIFF)])

    @pl.when(s_id == pl.num_programs(1) - 1)
    def _():
        s_new = lax.dot_general(q, kn_ref[0], (((1,), (1,)), ((), ())), preferred_element_type=F32)
        key = lax.broadcasted_iota(jnp.int32, s_new.shape, 1)
        tok = lax.broadcasted_iota(jnp.int32, s_new.shape, 0) % ts
        s_new = jnp.where(key <= tok, s_new, NEG_BIG)
        update(s_new, [vn_ref[0, :, h * DV_DIFF:(h + 1) * DV_DIFF] for h in range(H_DIFF)])
        lam = _diff_lambda(lq1_ref, lk1_ref, lq2_ref, lk2_ref)
        on = acc_sc[...] / l_sc[:, 0:1]
        for h in range(H_DIFF):
            o = on[h * hr:h * hr + ts, :] - lam * on[h * hr + ts:(h + 1) * hr, :]
            of = o * lax.rsqrt(jnp.mean(o * o, axis=-1, keepdims=True) + NORM_EPS) * (1.0 - LAM_INIT)
            o_ref[0, :, h * DV_DIFF:(h + 1) * DV_DIFF] = of


def _attn_sample(q_rows, k_new, v_new, cache_k, cache_v, page_table, lams, *, pages, ts):
    bs, n_pages = page_table.shape
    rows = q_rows.shape[1]
    steps = n_pages // pages
    lam_spec = pl.BlockSpec((1, DH_DIFF), lambda b, s, pt: (0, 0))

    assert 2 * ts == 8, "one head's query rows must fill one sublane tile"

    def page_spec(i):
        return pl.BlockSpec((1, W_SEC, V7X_LANES), lambda b, s, pt: (pt[b, s * pages + i], 0, 0))

    per_b = lambda r: pl.BlockSpec((1, r, W_SEC), lambda b, s, pt: (b, 0, 0))
    grid_spec = pltpu.PrefetchScalarGridSpec(
        num_scalar_prefetch=1,
        grid=(bs, steps),
        in_specs=[per_b(rows), per_b(PAGE_SIZE), per_b(PAGE_SIZE), lam_spec, lam_spec, lam_spec, lam_spec]
                 + [page_spec(i) for i in range(pages)] + [page_spec(i) for i in range(pages)],
        out_specs=per_b(ts),
        scratch_shapes=[pltpu.VMEM((rows, V7X_LANES), F32), pltpu.VMEM((rows, V7X_LANES), F32),
                        pltpu.VMEM((rows, DV_DIFF), F32)],
    )
    return pl.pallas_call(
        functools.partial(_decode_kernel, pages=pages, ts=ts),
        grid_spec=grid_spec,
        out_shape=jax.ShapeDtypeStruct((bs, ts, W_DIFF), F32),
        compiler_params=pltpu.CompilerParams(
            dimension_semantics=("arbitrary", "arbitrary"), vmem_limit_bytes=VMEM_LIMIT),
        name="attn_sample",
    )(page_table, q_rows, k_new, v_new, *lams, *([cache_k] * pages), *([cache_v] * pages))


def _retention_units(get, state, h, put, *, chunk, chunk_len, n_chunks):
    ii = lax.broadcasted_iota(jnp.int32, (chunk, chunk), 0)
    jj = lax.broadcasted_iota(jnp.int32, (chunk, chunk), 1)
    dif = (ii - jj).astype(F32)
    tok = lax.broadcasted_iota(jnp.int32, (chunk, DK_RET), 0).astype(F32)
    decay = jnp.where(dif >= 0, jnp.exp(jnp.maximum(dif, 0.0) * LOG_G[h]), 0.0)
    q_dec = jnp.exp((tok + 1.0) * LOG_G[h])
    k_dec = jnp.exp((chunk_len - 1.0 - tok) * LOG_G[h])
    chunk_dec = math.exp(chunk_len * LOG_G[h])

    def unit(c):
        rows = slice(c * chunk, (c + 1) * chunk)
        qc, kc, vc = get(0, rows), get(1, rows), get(2, rows)
        r_prev = state[h]
        s = lax.dot_general(qc, kc, (((1,), (1,)), ((), ())), preferred_element_type=F32) * decay
        o = (jnp.dot(s.astype(qc.dtype), vc, preferred_element_type=F32)
             + jnp.dot(qc, r_prev.astype(qc.dtype), preferred_element_type=F32) * q_dec)
        vd = (vc.astype(F32) * k_dec).astype(qc.dtype)
        kv = lax.dot_general(kc, vd, (((0,), (0,)), ((), ())), preferred_element_type=F32)
        state[h] = r_prev * chunk_dec + kv
        mu = jnp.mean(o, axis=-1, keepdims=True)
        var = jnp.mean(jnp.square(o - mu), axis=-1, keepdims=True)
        nrm = (o - mu) * lax.rsqrt(var + GN_EPS)
        put(rows, nrm * _silu(get(3, rows).astype(F32)))

    return [functools.partial(unit, c) for c in range(n_chunks)]


def _retention_kernel(q_ref, k_ref, v_ref, g_ref, r0_ref, o_ref, rout_ref, state, *, chunk, chunk_len, n_chunks):
    t = pl.program_id(1)
    refs = (q_ref, k_ref, v_ref, g_ref)

    @pl.when(t == 0)
    def _():
        state[...] = r0_ref[...]

    for bi in range(q_ref.shape[0]):
        for h in range(H_RET):
            cols = slice(h * DK_RET, (h + 1) * DK_RET)

            def get(i, rows, bi=bi, cols=cols):
                return refs[i][bi, rows, cols]

            def put(rows, val, bi=bi, cols=cols):
                o_ref[bi, rows, cols] = val.astype(o_ref.dtype)

            for unit in _retention_units(get, state.at[bi], h, put, chunk=chunk, chunk_len=chunk_len,
                                         n_chunks=n_chunks):
                unit()

    @pl.when(t == pl.num_programs(1) - 1)
    def _():
        rout_ref[...] = state[...]


def _retention(qr, kr, vr, gr, r0, *, chunk, chunk_len, n_chunks, seqs):
    b, t, _ = qr.shape
    tr = chunk * n_chunks
    tok_spec = pl.BlockSpec((seqs, tr, W_SEC), lambda b_, i: (b_, i, 0))
    st_spec = pl.BlockSpec((seqs, H_RET, DK_RET, DV_RET), lambda b_, i: (b_, 0, 0, 0))
    return pl.pallas_call(
        functools.partial(_retention_kernel, chunk=chunk, chunk_len=chunk_len, n_chunks=n_chunks),
        grid=(b // seqs, t // tr),
        in_specs=[tok_spec] * 4 + [st_spec],
        out_specs=[tok_spec, st_spec],
        out_shape=[jax.ShapeDtypeStruct((b, t, W_RET), BF16),
                   jax.ShapeDtypeStruct((b, H_RET, DK_RET, DV_RET), F32)],
        scratch_shapes=[pltpu.VMEM((seqs, H_RET, DK_RET, DV_RET), F32)],
        compiler_params=pltpu.CompilerParams(
            dimension_semantics=("arbitrary", "arbitrary"), vmem_limit_bytes=VMEM_LIMIT),
        name="retention",
    )(qr, kr, vr, gr, r0)


def _route(z):
    lane = lax.broadcasted_iota(jnp.int32, z.shape, 1).astype(F32)
    big = 1e9
    neg = -jnp.inf
    is_g = lane < N_GROUPS
    glog = jnp.where(is_g, z, neg)
    gmax = jnp.max(glog, axis=-1, keepdims=True)
    gsel = jnp.min(jnp.where(glog == gmax, lane, big), axis=-1, keepdims=True)
    psum = jnp.sum(jnp.where(is_g, jnp.exp(z - gmax), 0.0), axis=-1, keepdims=True)
    pg_sel = 1.0 / psum
    rel = jnp.zeros_like(z)
    for g in range(N_GROUPS):
        first = ROUTER_E0 + g * EXPERTS_PER_GROUP
        rel = jnp.where(gsel == float(g), pltpu.roll(z, V7X_LANES - first, 1), rel)
    elog = jnp.where(lane < EXPERTS_PER_GROUP, rel, neg)
    v1 = jnp.max(elog, axis=-1, keepdims=True)
    i1 = jnp.min(jnp.where(elog == v1, lane, big), axis=-1, keepdims=True)
    elog2 = jnp.where(lane == i1, neg, elog)
    v2 = jnp.max(elog2, axis=-1, keepdims=True)
    i2 = jnp.min(jnp.where(elog2 == v2, lane, big), axis=-1, keepdims=True)
    e2 = jnp.exp(v2 - v1)
    den = 1.0 + e2
    w1 = (1.0 / den) * pg_sel
    w2 = (e2 / den) * pg_sel
    return gsel, jnp.where(lane == i1, w1, jnp.where(lane == i2, w2, 0.0))


MOE_CHUNK = 160
MOE_PAD = 256


def _ffn_kernel(x_ref, od_ref, rf_ref, beta_ref, wout_ref, gt1_ref, gffn_ref, sc2_ref, sh2_ref, gt2_ref,
                wr_ref, br_ref, tri_ref, wgu_ref, wd_ref, gfin_ref, y_ref,
                h2_sc, pos_sc, posr_sc, moe_sc, *, chunk):
    tm = x_ref.shape[1]
    beta = beta_ref[...]
    yd = (od_ref[0].astype(F32) * beta[:, :W_DIFF]).astype(BF16)
    yr = (rf_ref[0].astype(F32) * beta[:, W_DIFF:]).astype(BF16)
    mix = (jnp.dot(yd, wout_ref[:W_DIFF, :], preferred_element_type=F32)
           + jnp.dot(yr, wout_ref[W_DIFF:, :], preferred_element_type=F32))
    x1 = x_ref[0] + gt1_ref[0] * mix
    n2 = x1 * lax.rsqrt(jnp.mean(x1 * x1, axis=-1, keepdims=True) + NORM_EPS)
    h2 = (n2 * gffn_ref[...]) * (1.0 + sc2_ref[0]) + sh2_ref[0]
    h_hi = h2.astype(BF16)
    h_lo = (h2 - h_hi.astype(F32)).astype(BF16)
    zz = jnp.dot(h_hi, wr_ref[...], preferred_element_type=F32)
    z = (zz[:, :V7X_LANES] + zz[:, V7X_LANES:]
         + jnp.dot(h_lo, wr_ref[:, :V7X_LANES], preferred_element_type=F32) + br_ref[...])
    gsel, gates = _route(z)
    d_model = h2.shape[1]
    g_hi = gates.astype(BF16)
    h2_sc[:, :d_model] = h_hi
    h2_sc[:, d_model:d_model + V7X_LANES] = g_hi
    h2_sc[:, d_model + V7X_LANES:] = (gates - g_hi.astype(F32)).astype(BF16)

    lane = lax.broadcasted_iota(jnp.int32, (tm, V7X_LANES), 1).astype(F32)
    lane1 = lax.broadcasted_iota(jnp.int32, (1, V7X_LANES), 1)
    onehot = jnp.where(lane == gsel, 1.0, 0.0)
    earlier = jnp.dot(tri_ref[...], onehot.astype(BF16), preferred_element_type=F32)
    rank = jnp.sum(onehot * earlier, axis=-1, keepdims=True)
    cnt = jnp.sum(onehot, axis=0, keepdims=True)
    n_chunks = jnp.zeros_like(cnt)
    for m in range(-(-tm // chunk)):
        n_chunks = n_chunks + jnp.where(cnt > m * chunk, 1.0, 0.0)
    first_chunk = [jnp.int32(0)]
    for g in range(N_GROUPS):
        first_chunk.append(first_chunk[-1] + n_chunks[0, g].astype(jnp.int32))
    first_vec = jnp.zeros((1, V7X_LANES), F32)
    for g in range(N_GROUPS):
        first_vec = jnp.where(lane1 == g, first_chunk[g].astype(F32), first_vec)
    pos = jnp.sum(onehot * first_vec, axis=-1, keepdims=True) * chunk + rank
    pos_sc[...] = jnp.broadcast_to(pos, pos_sc.shape)
    posr_sc[...] = jnp.broadcast_to(pos, (tm, V7X_LANES)).T[0:8, :]
    moe_sc[...] = jnp.zeros(moe_sc.shape, F32)

    col_id = lax.broadcasted_iota(jnp.int32, (tm, MOE_PAD), 1)
    col_f = col_id.astype(F32)
    row_f = lax.broadcasted_iota(jnp.int32, (chunk, tm), 0).astype(F32)

    def chunk_body(k, carry):
        g = (jnp.where(k >= first_chunk[1], 1, 0) + jnp.where(k >= first_chunk[2], 1, 0)
             + jnp.where(k >= first_chunk[3], 1, 0))
        off = (k * chunk).astype(F32)
        gather = jnp.where(posr_sc[0:1, :] - off == row_f, 1.0, 0.0).astype(BF16)
        scatter = jnp.where((pos_sc[...] - off == col_f) & (col_id < chunk), 1.0, 0.0).astype(BF16)
        gathered = jnp.dot(gather, h2_sc[...], preferred_element_type=F32)
        hc = gathered[:, :d_model].astype(BF16)
        gc = gathered[:, d_model:d_model + V7X_LANES] + gathered[:, d_model + V7X_LANES:]
        hh = []
        for e4 in range(EXPERTS_PER_GROUP):
            au = jnp.dot(hc, wgu_ref[g * EXPERTS_PER_GROUP + e4], preferred_element_type=F32)
            a, u = au[:, :D_EXPERT], au[:, D_EXPERT:]
            hh.append((_silu(a) * u * gc[:, e4:e4 + 1]).astype(BF16))
        oc = jnp.dot(jnp.concatenate(hh, axis=1), wd_ref[g], preferred_element_type=F32)
        oc = jnp.concatenate([oc.astype(BF16), jnp.zeros((MOE_PAD - chunk, oc.shape[1]), BF16)], axis=0)
        moe_sc[...] += jnp.dot(scatter, oc, preferred_element_type=F32)
        return carry

    lax.fori_loop(0, first_chunk[N_GROUPS], chunk_body, 0)
    x2 = x1 + gt2_ref[0] * moe_sc[...]
    y_ref[0] = (x2 * lax.rsqrt(jnp.mean(x2 * x2, axis=-1, keepdims=True) + NORM_EPS)) * gfin_ref[...]


def _ffn(x, od, rf, mod, weights, *, tm):
    bx, tx, d = x.shape
    beta, w_out_b, g_ffn, w_router, b_router, wgu_b, wd_b, g_fin = weights
    chunk = min(MOE_CHUNK, tm)
    tri = jnp.asarray(np.tril(np.ones((tm, tm), np.float32), -1), BF16)
    const = lambda shape: pl.BlockSpec(shape, lambda b, t: (0,) * len(shape), pipeline_mode=pl.Buffered(1))
    tok = lambda w: pl.BlockSpec((1, tm, w), lambda b, t: (b, t, 0))
    gw = EXPERTS_PER_GROUP * D_EXPERT
    return pl.pallas_call(
        functools.partial(_ffn_kernel, chunk=chunk),
        grid=(bx, tx // tm),
        in_specs=[tok(d), tok(W_DIFF), tok(W_RET), const((1, d)), const((d, d)), _mod_spec(mod, d, MOD_GT1),
                  const((1, d)), _mod_spec(mod, d, MOD_SC2), _mod_spec(mod, d, MOD_SH2), _mod_spec(mod, d, MOD_GT2),
                  const((d, 2 * V7X_LANES)), const((1, V7X_LANES)),
                  const((tm, tm)), const((N_EXPERTS, d, 2 * D_EXPERT)), const((N_GROUPS, gw, d)), const((1, d))],
        out_specs=tok(d),
        out_shape=jax.ShapeDtypeStruct((bx, tx, d), F32),
        scratch_shapes=[pltpu.VMEM((tm, d + 2 * V7X_LANES), BF16),
                        pltpu.VMEM((tm, MOE_PAD), F32), pltpu.VMEM((8, tm), F32),
                        pltpu.VMEM((tm, d), F32)],
        compiler_params=pltpu.CompilerParams(
            dimension_semantics=("arbitrary", "arbitrary"), vmem_limit_bytes=VMEM_LIMIT),
        name="ffn",
    )(x, od, rf, beta, w_out_b, mod, g_ffn, mod, mod, mod, w_router, b_router, tri, wgu_b, wd_b, g_fin)


def kernel(x_prompt, x_sample, cache_k, cache_v, state_ret, page_table, c_prompt, c_sample, w_ada, b_ada,
           norm_mix_g, norm_ffn_g, w_in, lambda_q1, lambda_k1, lambda_q2, lambda_k2, beta_mix, w_out,
           w_group, b_group, w_expert_router, b_expert_router, w_gate_e, w_up_e, w_down_e, final_g):
    assert w_ada.shape[0] == 1, "single-layer stack only"
    bp, tp, d = x_prompt.shape
    bs, ts, _ = x_sample.shape
    n_pages = page_table.shape[1]
    past_len = n_pages * PAGE_SIZE
    n_phys = cache_k.shape[1]

    w_in_b = w_in[0].astype(BF16)
    w_out_b = w_out[0].astype(BF16)
    wgu_b = jnp.concatenate([w_gate_e[0], w_up_e[0]], axis=-1).astype(BF16)
    wd_b = w_down_e[0].reshape(N_GROUPS, EXPERTS_PER_GROUP * D_EXPERT, d).astype(BF16)
    w_er = jnp.transpose(w_expert_router[0], (1, 0, 2)).reshape(d, N_EXPERTS)
    pad = V7X_LANES - N_GROUPS - N_EXPERTS
    w_router = jnp.concatenate([w_group[0], w_er, jnp.zeros((d, pad), F32)], axis=1)
    w_router_hi = w_router.astype(BF16)
    w_router = jnp.concatenate([w_router_hi, (w_router - w_router_hi.astype(F32)).astype(BF16)], axis=1)
    b_router = jnp.concatenate([b_group[0], b_expert_router[0].reshape(N_EXPERTS), jnp.zeros((pad,), F32)])[None]
    ffn_weights = (beta_mix, w_out_b, norm_ffn_g, w_router, b_router, wgu_b, wd_b, final_g[None])
    lams = (lambda_q1, lambda_k1, lambda_q2, lambda_k2)

    mod = _adaln(jnp.concatenate([c_prompt, c_sample], axis=0), w_ada[0], b_ada[0])
    mod_p = mod[:bp, None, :]
    mod_s = jnp.repeat(mod[bp:], ts, axis=0)[None]

    tabs_p = _rope_tables(np.arange(tp))
    kp, vp, qT, kb, vT, rf_p, ret_p = _inproj(
        x_prompt, norm_mix_g, mod_p, w_in_b, tabs_p, tm=TM_PROMPT, transposed=True, fuse_retention=True)
    od_p = _attn_prompt(qT, kb, vT, lams, tq=ATTN_TQ, heads=ATTN_HEADS)
    y_p = _ffn(x_prompt, od_p, rf_p, mod_p, ffn_weights, tm=TM_PROMPT)

    rows_s = bs * ts
    tabs_s = _rope_tables(np.tile(past_len + np.arange(ts), bs))
    ks, vs, q_s, kb_s, vb_s, qr_s, kr_s, vr_s, gr_s = _inproj(
        x_sample.reshape(1, rows_s, d), norm_mix_g, mod_s, w_in_b, tabs_s, tm=rows_s, transposed=False,
        fuse_retention=False)
    sub = jnp.arange(2 * H_DIFF)
    col_sub = jnp.arange(W_SEC) // DH_DIFF
    q_rows = jnp.where((sub[:, None, None] == col_sub[None, None, :])[None],
                       q_s.reshape(bs, 1, ts, W_SEC), jnp.zeros((), BF16)).reshape(bs, 2 * H_DIFF * ts, W_SEC)
    pad_keys = lambda a: jnp.pad(a.reshape(bs, ts, W_SEC), ((0, 0), (0, PAGE_SIZE - ts), (0, 0)))
    ck = jnp.transpose(cache_k[0], (0, 2, 3, 1)).reshape(n_phys, 2 * H_DIFF * DH_DIFF, PAGE_SIZE)
    cv = cache_v[0].reshape(n_phys, PAGE_SIZE * H_DIFF, DV_DIFF)
    od_s = _attn_sample(q_rows, pad_keys(kb_s), pad_keys(vb_s), ck, cv, page_table, lams, pages=DECODE_PAGES, ts=ts)
    pad_tok = lambda a: jnp.pad(a.reshape(bs, ts, W_SEC), ((0, 0), (0, SAMPLE_RET_ROWS - ts), (0, 0)))
    rf_s, ret_s = _retention(pad_tok(qr_s), pad_tok(kr_s), pad_tok(vr_s), pad_tok(gr_s), state_ret[0],
                             chunk=SAMPLE_RET_ROWS, chunk_len=ts, n_chunks=1, seqs=SAMPLE_RET_SEQS)
    y_s = _ffn(x_sample.reshape(1, rows_s, d), od_s.reshape(1, rows_s, W_SEC),
               rf_s[:, :ts].reshape(1, rows_s, W_SEC), mod_s, ffn_weights, tm=rows_s)

    return (y_p, y_s.reshape(bs, ts, d),
            kp.reshape(1, bp, tp, 2 * H_DIFF, DH_DIFF), vp.reshape(1, bp, tp, H_DIFF, DV_DIFF), ret_p[None],
            ks.reshape(1, bs, ts, 2 * H_DIFF, DH_DIFF), vs.reshape(1, bs, ts, H_DIFF, DV_DIFF), ret_s[None])
```
